```python
import math
import jax, jax.numpy as jnp
from jax import lax
import numpy as np

D_MODEL = 2048
BATCH = 4
SEQ = 4096
DEPTH = 1

CHUNK = 64
D_MIX = D_MODEL
NORM_EPS = 1e-6
DSA_WIDTH = D_MIX // 2
DSA_HEAD_DIM = 128
DSA_HEADS = DSA_WIDTH // DSA_HEAD_DIM
IDX_HEADS = 16
IDX_DIM = 64
TOPK_MAX = 256
ROPE_THETA = 10000.0
RWKV_WIDTH = D_MIX - DSA_WIDTH
RWKV_HEAD_DIM = 64
RWKV_HEADS = RWKV_WIDTH // RWKV_HEAD_DIM
DECAY_LORA = 64
AICL_LORA = 64
GATE_LORA = 160
GN_EPS = 64e-5
N_GROUPS = 4
EXPERTS_PER_GROUP = 4
N_EXPERTS = N_GROUPS * EXPERTS_PER_GROUP
TOP_K_IN_GROUP = 2
D_EXPERT = 512
DSA_SPLITS = (DSA_WIDTH, DSA_WIDTH, DSA_WIDTH, IDX_HEADS * IDX_DIM, IDX_DIM, IDX_HEADS)
RWKV_SPLITS = (RWKV_WIDTH, DECAY_LORA, RWKV_WIDTH, RWKV_WIDTH, AICL_LORA, GATE_LORA)
DSA_COLS = sum(DSA_SPLITS)
RWKV_COLS = sum(RWKV_SPLITS)
IN_COLS = DSA_COLS + RWKV_COLS

kernel_name = 'hybrid_dsa_rwkv7_hmoe_block'


def split_cols(p, sizes):
    cuts = np.cumsum(sizes)[:-1].tolist()
    return jnp.split(p, cuts, axis=-1)


def rms_norm(x, g, eps=NORM_EPS):
    xf = x.astype(jnp.float32)
    y = xf * lax.rsqrt(jnp.mean(xf * xf, axis=-1, keepdims=True) + eps)
    return (y * g.astype(jnp.float32)).astype(x.dtype)


def rope(x, pos):
    d = x.shape[-1]
    half = d // 2
    inv = ROPE_THETA ** (-jnp.arange(half, dtype=jnp.float32) * (2.0 / d))
    ang = pos.astype(jnp.float32)[:, None] * inv[None, :]
    cos = jnp.cos(ang)[None, :, None, :]
    sin = jnp.sin(ang)[None, :, None, :]
    xf = x.astype(jnp.float32)
    x1, x2 = xf[..., :half], xf[..., half:]
    out = jnp.concatenate([x1 * cos - x2 * sin, x2 * cos + x1 * sin], axis=-1)
    return out.astype(x.dtype)


def token_shift_mix(p, mu):
    prev = jnp.pad(p, ((0, 0), (1, 0), (0, 0)))[:, :-1]
    return p + (prev - p) * mu


def dsa_mixer(q, k, v, q_idx, k_idx, w_idx, q_gain, k_gain):
    B, S = q.shape[0], q.shape[1]
    f32 = jnp.float32
    pos = jnp.arange(S)
    q = rope(rms_norm(q.reshape(B, S, DSA_HEADS, DSA_HEAD_DIM), q_gain), pos)
    k = rope(rms_norm(k.reshape(B, S, DSA_HEADS, DSA_HEAD_DIM), k_gain), pos)
    v = v.reshape(B, S, DSA_HEADS, DSA_HEAD_DIM)
    q_idx = rope(q_idx.reshape(B, S, IDX_HEADS, IDX_DIM), pos)
    k_idx = rope(k_idx[:, :, None, :], pos)[:, :, 0]
    w_idx = w_idx.astype(f32) * (IDX_HEADS ** -0.5 * IDX_DIM ** -0.5)
    top_k = min(TOPK_MAX, S // 4)
    n_blocks = S // CHUNK
    key_pos = jnp.arange(S)
    scale = DSA_HEAD_DIM ** -0.5

    def block(i):
        start = i * CHUNK
        qi_idx = lax.dynamic_slice_in_dim(q_idx, start, CHUNK, axis=1)
        wi = lax.dynamic_slice_in_dim(w_idx, start, CHUNK, axis=1)
        qi = lax.dynamic_slice_in_dim(q, start, CHUNK, axis=1)
        iscore = jnp.einsum('bqhd,bsd->bqhs', qi_idx, k_idx, preferred_element_type=f32)
        iscore = jnp.einsum('bqhs,bqh->bqs', jax.nn.relu(iscore), wi)
        admissible = key_pos < start + CHUNK
        iscore = jnp.where(admissible[None, None, :], iscore, -jnp.inf)
        sel_score, sel = lax.top_k(iscore, top_k)
        valid = jnp.isfinite(sel_score)
        k_sel = jax.vmap(lambda kb, ib: kb[ib])(k, sel)
        v_sel = jax.vmap(lambda vb, ib: vb[ib])(v, sel)
        logits = jnp.einsum('bqhd,bqkhd->bhqk', qi, k_sel, preferred_element_type=f32) * scale
        logits = jnp.where(valid[:, None, :, :], logits, -jnp.inf)
        probs = jax.nn.softmax(logits, axis=-1)
        return jnp.einsum('bhqk,bqkhd->bqhd', probs.astype(v.dtype), v_sel)

    out = lax.map(block, jnp.arange(n_blocks))
    return out.transpose(1, 0, 2, 3, 4).reshape(B, S, DSA_WIDTH)


def rwkv7_mixer(r, d_w, k, v, d_a, d_g, w0, w_decay_up, a0, w_aicl_up, w_gate_lora_up,
                k_k, k_a, r_k, ln_x_w, ln_x_b, out_dtype):
    B, S = r.shape[0], r.shape[1]
    H, N = RWKV_HEADS, RWKV_HEAD_DIM
    f32 = jnp.float32
    w = -jax.nn.softplus(-(w0 + jnp.tanh(d_w) @ w_decay_up)) - 0.5
    decay = jnp.exp(-jnp.exp(w.astype(f32)))
    a = jax.nn.sigmoid((a0 + d_a @ w_aicl_up).astype(f32))
    g = (jax.nn.sigmoid(d_g) @ w_gate_lora_up).astype(f32)
    heads = lambda t: t.astype(f32).reshape(B, S, H, N)
    kk = heads(k * k_k)
    kk = kk / jnp.maximum(jnp.sqrt(jnp.sum(kk * kk, axis=-1, keepdims=True)), 1e-12)
    k = k.astype(f32) * (1.0 + (a - 1.0) * k_a.astype(f32))
    r_h, k_h, v_h, w_h, a_h = heads(r), heads(k), heads(v), heads(decay), heads(a)
    b_h = kk * a_h

    def step(state, inp):
        r_t, w_t, k_t, v_t, kk_t, b_t = inp
        sa = jnp.einsum('bhvk,bhk->bhv', state, -kk_t)
        state = (state * w_t[:, :, None, :] + sa[..., None] * b_t[:, :, None, :]
                 + v_t[..., None] * k_t[:, :, None, :])
        return state, jnp.einsum('bhvk,bhk->bhv', state, r_t)

    xs = tuple(t.transpose(1, 0, 2, 3) for t in (r_h, w_h, k_h, v_h, kk, b_h))
    _, ys = lax.scan(step, jnp.zeros((B, H, N, N), f32), xs)
    y = ys.transpose(1, 0, 2, 3)
    mu = jnp.mean(y, axis=-1, keepdims=True)
    var = jnp.mean(jnp.square(y - mu), axis=-1, keepdims=True)
    y = ((y - mu) * lax.rsqrt(var + GN_EPS)).reshape(B, S, RWKV_WIDTH)
    y = y * ln_x_w.astype(f32) + ln_x_b.astype(f32)
    bonus = jnp.sum(r_h * k_h * r_k.astype(f32), axis=-1, keepdims=True) * v_h
    y = (y + bonus.reshape(B, S, RWKV_WIDTH)) * g
    return y.astype(out_dtype)


def hier_moe(h, w_route_group, b_route_group, w_route_expert, b_route_expert,
             w_e_gate, w_e_up, w_e_down):
    B, S, D = h.shape
    f32 = jnp.float32
    t = h.reshape(B * S, D)
    grp_prob = jax.nn.softmax((t @ w_route_group).astype(f32) + b_route_group.astype(f32), axis=-1)
    p_grp, g_sel = lax.top_k(grp_prob, 1)
    exp_logits = ((t @ w_route_expert).astype(f32) + b_route_expert.astype(f32))
    exp_logits = exp_logits.reshape(-1, N_GROUPS, EXPERTS_PER_GROUP)
    exp_logits = jnp.take_along_axis(exp_logits, g_sel[:, :, None], axis=1)[:, 0]
    p_exp, e_sel = lax.top_k(jax.nn.softmax(exp_logits, axis=-1), TOP_K_IN_GROUP)
    p_exp = p_exp / jnp.sum(p_exp, axis=-1, keepdims=True)
    gate_w = p_grp * p_exp
    expert_id = g_sel * EXPERTS_PER_GROUP + e_sel
    combine = jnp.sum(jax.nn.one_hot(expert_id, N_EXPERTS, dtype=f32) * gate_w[..., None], axis=1)
    y = jnp.zeros((B * S, D), f32)
    for e in range(N_EXPERTS):
        he = jax.nn.silu(t @ w_e_gate[e]) * (t @ w_e_up[e])
        y = y + combine[:, e:e + 1] * (he @ w_e_down[e]).astype(f32)
    return y.reshape(B, S, D).astype(h.dtype)


def setup_inputs(seed: int = 0) -> dict:
    key = jax.random.key(seed)
    ks = jax.random.split(key, 32)
    f32 = jnp.float32
    nrm = lambda k, shape, s: jax.random.normal(k, shape, f32) * s
    L, D = DEPTH, D_MODEL
    return {
        'x': jax.random.normal(ks[0], (BATCH, SEQ, D), f32),
        'g_mix': 1.0 + nrm(ks[1], (L, D), 0.02),
        'w_in': nrm(ks[2], (L, D, IN_COLS), D ** -0.5),
        'rwkv_shift_mix': jax.random.uniform(ks[3], (L, RWKV_COLS), f32),
        'q_gain': 1.0 + nrm(ks[4], (L, DSA_HEAD_DIM), 0.02),
        'k_gain': 1.0 + nrm(ks[5], (L, DSA_HEAD_DIM), 0.02),
        'w0': jax.random.uniform(ks[6], (L, RWKV_WIDTH), f32, -6.0, -1.0),
        'w_decay_up': nrm(ks[7], (L, DECAY_LORA, RWKV_WIDTH), 0.1),
        'a0': nrm(ks[8], (L, RWKV_WIDTH), 0.1),
        'w_aicl_up': nrm(ks[9], (L, AICL_LORA, RWKV_WIDTH), 0.5 * AICL_LORA ** -0.5),
        'w_gate_lora_up': nrm(ks[10], (L, GATE_LORA, RWKV_WIDTH), GATE_LORA ** -0.5),
        'k_k': 0.85 + nrm(ks[11], (L, RWKV_WIDTH), 0.02),
        'k_a': 1.0 + nrm(ks[12], (L, RWKV_WIDTH), 0.02),
        'r_k': nrm(ks[13], (L, RWKV_HEADS, RWKV_HEAD_DIM), 0.1),
        'ln_x_w': 1.0 + nrm(ks[14], (L, RWKV_WIDTH), 0.02),
        'ln_x_b': nrm(ks[15], (L, RWKV_WIDTH), 0.01),
        'w_out': nrm(ks[16], (L, D_MIX, D), D_MIX ** -0.5),
        'g_ffn': 1.0 + nrm(ks[17], (L, D), 0.02),
        'w_route_group': nrm(ks[18], (L, D, N_GROUPS), D ** -0.5),
        'b_route_group': nrm(ks[19], (L, N_GROUPS), 0.01),
        'w_route_expert': nrm(ks[20], (L, D, N_EXPERTS), D ** -0.5),
        'b_route_expert': nrm(ks[21], (L, N_EXPERTS), 0.01),
        'w_e_gate': nrm(ks[22], (L, N_EXPERTS, D, D_EXPERT), D ** -0.5),
        'w_e_up': nrm(ks[23], (L, N_EXPERTS, D, D_EXPERT), D ** -0.5),
        'w_e_down': nrm(ks[24], (L, N_EXPERTS, D_EXPERT, D), D_EXPERT ** -0.5),
    }


def reference(x, g_mix, w_in, rwkv_shift_mix, q_gain, k_gain, w0, w_decay_up, a0, w_aicl_up,
              w_gate_lora_up, k_k, k_a, r_k, ln_x_w, ln_x_b, w_out, g_ffn, w_route_group,
              b_route_group, w_route_expert, b_route_expert, w_e_gate, w_e_up, w_e_down):
    for l in range(DEPTH):
        h = rms_norm(x, g_mix[l])
        proj = h @ w_in[l]
        p_dsa, p_rwkv = proj[..., :DSA_COLS], proj[..., DSA_COLS:]
        q, k, v, q_idx, k_idx, w_idx = split_cols(p_dsa, DSA_SPLITS)
        y_dsa = dsa_mixer(q, k, v, q_idx, k_idx, w_idx, q_gain[l], k_gain[l])
        p_rwkv = token_shift_mix(p_rwkv, rwkv_shift_mix[l])
        r, d_w, kr, vr, d_a, d_g = split_cols(p_rwkv, RWKV_SPLITS)
        y_rwkv = rwkv7_mixer(r, d_w, kr, vr, d_a, d_g, w0[l], w_decay_up[l], a0[l], w_aicl_up[l],
                             w_gate_lora_up[l], k_k[l], k_a[l], r_k[l], ln_x_w[l], ln_x_b[l],
                             x.dtype)
        x = x + jnp.concatenate([y_dsa.astype(x.dtype), y_rwkv], axis=-1) @ w_out[l]
        x = x + hier_moe(rms_norm(x, g_ffn[l]), w_route_group[l], b_route_group[l],
                         w_route_expert[l], b_route_expert[l], w_e_gate[l], w_e_up[l],
                         w_e_down[l])
    return x
```

```python
import functools

import jax
import jax.numpy as jnp
import numpy as np
from jax import lax
from jax.experimental import pallas as pl
from jax.experimental.pallas import tpu as pltpu

F32 = jnp.float32
BF16 = jnp.bfloat16

NORM_EPS = 1e-6
CHUNK = 64
DSA_HEADS = 8
DSA_HEAD_DIM = 128
DSA_WIDTH = DSA_HEADS * DSA_HEAD_DIM
IDX_HEADS = 16
IDX_DIM = 64
TOPK_MAX = 256
ROPE_THETA = 10000.0
RWKV_HEADS = 16
RWKV_HEAD_DIM = 64
RWKV_WIDTH = RWKV_HEADS * RWKV_HEAD_DIM
DECAY_LORA = 64
AICL_LORA = 64
GATE_LORA = 160
GN_EPS = 64e-5
N_GROUPS = 4
EXPERTS_PER_GROUP = 4
N_EXPERTS = 16
D_EXPERT = 512

LANES = 128
VMEM_LIMIT = 56 * 1024 * 1024

DSA_PAD = 4352
RWKV_PAD = 3584
DSA_SMALL = 4096
RWKV_SMALL = 3072

NEG_INF = float("-inf")
INT_MIN = -(2 ** 31)


def _cparams(sem):
    return pltpu.CompilerParams(dimension_semantics=sem, vmem_limit_bytes=VMEM_LIMIT)


def _norm_matmul_kernel(x_ref, g_ref, w_ref, o_ref, h_ref):
    @pl.when(pl.program_id(1) == 0)
    def _():
        x = x_ref[...]
        ms = jnp.mean(x * x, axis=-1, keepdims=True)
        h_ref[...] = (x * lax.rsqrt(ms + NORM_EPS) * g_ref[...]).astype(BF16)

    o_ref[...] = jnp.dot(h_ref[...], w_ref[...], preferred_element_type=F32)


def norm_matmul(x, g, w, tm, tn):
    T, D = x.shape
    N = w.shape[1]
    return pl.pallas_call(
        _norm_matmul_kernel,
        grid=(T // tm, N // tn),
        in_specs=[
            pl.BlockSpec((tm, D), lambda i, j: (i, 0)),
            pl.BlockSpec((1, D), lambda i, j: (0, 0)),
            pl.BlockSpec((D, tn), lambda i, j: (0, j)),
        ],
        out_specs=pl.BlockSpec((tm, tn), lambda i, j: (i, j)),
        out_shape=jax.ShapeDtypeStruct((T, N), F32),
        scratch_shapes=[pltpu.VMEM((tm, D), BF16)],
        compiler_params=_cparams(("parallel", "arbitrary")),
        name="norm_matmul",
    )(x, g, w)


def _rope_tables(S, d):
    half = d // 2
    inv = ROPE_THETA ** (-jnp.arange(half, dtype=F32) * (2.0 / d))
    ang = jnp.arange(S, dtype=F32)[:, None] * inv[None, :]
    cos, sin = jnp.cos(ang), jnp.sin(ang)
    return jnp.concatenate([cos, cos], axis=-1), jnp.concatenate([-sin, sin], axis=-1)


def _dsa_prep_kernel(p_ref, qg_ref, kg_ref, c128_ref, s128_ref, c64_ref, s64_ref,
                     q_ref, k_ref, v_ref, qi_ref, ki_ref, w_ref):
    c128, s128 = c128_ref[...], s128_ref[...]
    c64, s64 = c64_ref[...], s64_ref[...]
    tm = c128.shape[0]
    first_half = (lax.broadcasted_iota(jnp.int32, (tm, LANES), 1) % IDX_DIM) < IDX_DIM // 2

    def rope128(y):
        return y * c128 + pltpu.roll(y, DSA_HEAD_DIM // 2, 1) * s128

    def rope64(y):
        partner = jnp.where(first_half, pltpu.roll(y, LANES - IDX_DIM // 2, 1),
                            pltpu.roll(y, IDX_DIM // 2, 1))
        return y * c64 + partner * s64

    def normed(x, gain):
        ms = jnp.mean(x * x, axis=-1, keepdims=True)
        return x * lax.rsqrt(ms + NORM_EPS) * gain

    scale = DSA_HEAD_DIM ** -0.5
    for h in range(DSA_HEADS):
        lo = h * DSA_HEAD_DIM
        q = p_ref[:, lo:lo + DSA_HEAD_DIM]
        q_ref[:, lo:lo + DSA_HEAD_DIM] = (rope128(normed(q, qg_ref[...])) * scale).astype(BF16)
        k = p_ref[:, DSA_WIDTH + lo:DSA_WIDTH + lo + DSA_HEAD_DIM]
        k_ref[:, lo:lo + DSA_HEAD_DIM] = rope128(normed(k, kg_ref[...])).astype(BF16)
    v_ref[...] = p_ref[:, 2 * DSA_WIDTH:3 * DSA_WIDTH].astype(BF16)
    for j in range(IDX_HEADS * IDX_DIM // LANES):
        lo = 3 * DSA_WIDTH + j * LANES
        qi_ref[:, j * LANES:(j + 1) * LANES] = rope64(p_ref[:, lo:lo + LANES]).astype(BF16)
    small = p_ref[:, DSA_SMALL:DSA_SMALL + LANES]
    lane = lax.broadcasted_iota(jnp.int32, (tm, LANES), 1)
    ki = rope64(small)
    ki_ref[:, :LANES] = jnp.where(lane < IDX_DIM, ki, 0.0).astype(BF16)
    ki_ref[:, LANES:] = jnp.where(lane >= IDX_DIM, pltpu.roll(ki, IDX_DIM, 1), 0.0).astype(BF16)
    w_ref[...] = small * (IDX_HEADS ** -0.5 * IDX_DIM ** -0.5)


def dsa_prep(p_dsa, q_gain, k_gain, B, S, tm):
    T = B * S
    c128, s128 = _rope_tables(S, DSA_HEAD_DIM)
    c64, s64 = _rope_tables(S, IDX_DIM)
    c64 = jnp.concatenate([c64, c64], axis=-1)
    s64 = jnp.concatenate([s64, s64], axis=-1)
    nsb = S // tm
    row = lambda i: (i, 0)
    pos = lambda i: (i % nsb, 0)
    fixed = lambda i: (0, 0)
    return pl.pallas_call(
        _dsa_prep_kernel,
        grid=(T // tm,),
        in_specs=[
            pl.BlockSpec((tm, DSA_PAD), row),
            pl.BlockSpec((1, DSA_HEAD_DIM), fixed),
            pl.BlockSpec((1, DSA_HEAD_DIM), fixed),
            pl.BlockSpec((tm, LANES), pos),
            pl.BlockSpec((tm, LANES), pos),
            pl.BlockSpec((tm, LANES), pos),
            pl.BlockSpec((tm, LANES), pos),
        ],
        out_specs=[
            pl.BlockSpec((tm, DSA_WIDTH), row),
            pl.BlockSpec((tm, DSA_WIDTH), row),
            pl.BlockSpec((tm, DSA_WIDTH), row),
            pl.BlockSpec((tm, IDX_HEADS * IDX_DIM), row),
            pl.BlockSpec((tm, 2 * LANES), row),
            pl.BlockSpec((tm, LANES), row),
        ],
        out_shape=[
            jax.ShapeDtypeStruct((T, DSA_WIDTH), BF16),
            jax.ShapeDtypeStruct((T, DSA_WIDTH), BF16),
            jax.ShapeDtypeStruct((T, DSA_WIDTH), BF16),
            jax.ShapeDtypeStruct((T, IDX_HEADS * IDX_DIM), BF16),
            jax.ShapeDtypeStruct((T, 2 * LANES), BF16),
            jax.ShapeDtypeStruct((T, LANES), F32),
        ],
        compiler_params=_cparams(("parallel",)),
        name="dsa_prep",
    )(p_dsa, q_gain, k_gain, c128, s128, c64, s64)


KEY_NEG_INF = -2139095041


def _nt_dot(a, b):
    return lax.dot_general(a, b, (((1,), (1,)), ((), ())), preferred_element_type=F32)


def _dsa_kernel(q_ref, k_ref, v_ref, qi_ref, ki_ref, w_ref, y_ref, keys_ref, bias_ref,
                *, tq, tk, top_k):
    i = pl.program_id(1)
    nkb = ((i + 1) * tq + tk - 1) // tk
    row = i * tq + lax.broadcasted_iota(jnp.int32, (tq, 1), 0)
    limit = (row // CHUNK + 1) * CHUNK
    col0 = lax.broadcasted_iota(jnp.int32, (tq, tk), 1)

    w = w_ref[...]
    wcols = [w[:, IDX_DIM + h:IDX_DIM + h + 1] for h in range(IDX_HEADS)]

    def score_block(kb, carry):
        kblk = ki_ref[pl.ds(pl.multiple_of(kb * tk, tk), tk), :]
        acc = jnp.zeros((tq, tk), F32)
        for j in range(IDX_HEADS // 2):
            qblk = qi_ref[:, j * LANES:(j + 1) * LANES]
            s_even = _nt_dot(qblk, kblk[:, :LANES])
            s_odd = _nt_dot(qblk, kblk[:, LANES:])
            acc = acc + jnp.maximum(s_even, 0.0) * wcols[2 * j]
            acc = acc + jnp.maximum(s_odd, 0.0) * wcols[2 * j + 1]
        acc = jnp.where(acc == 0.0, 0.0, acc)
        acc = jnp.where(col0 + kb * tk < limit, acc, NEG_INF)
        bits = pltpu.bitcast(acc, jnp.int32)
        keys = jnp.where(bits < 0, bits ^ jnp.int32(0x7FFFFFFF), bits)
        keys_ref[:, pl.ds(pl.multiple_of(kb * tk, tk), tk)] = keys
        return carry

    lax.fori_loop(0, nkb, score_block, 0)

    def count_ge(cand):
        def body(kb, acc):
            m = (keys_ref[:, pl.ds(pl.multiple_of(kb * tk, tk), tk)] >= cand).astype(jnp.int32)
            for c in range(tk // LANES):
                acc = acc + m[:, c * LANES:(c + 1) * LANES]
            return acc
        part = lax.fori_loop(0, nkb, body, jnp.zeros((tq, LANES), jnp.int32))
        return jnp.sum(part, axis=1, keepdims=True)

    def bit_step(b, thr):
        cand = thr ^ jnp.left_shift(jnp.int32(1), 31 - b)
        return jnp.where(count_ge(cand) >= top_k, cand, thr)

    thr = lax.fori_loop(0, 32, bit_step, jnp.full((tq, 1), INT_MIN, jnp.int32))
    n_ge = count_ge(thr)
    tied = jnp.logical_and(n_ge > top_k, thr != KEY_NEG_INF)
    any_tied = jnp.max(tied.astype(jnp.int32)) > 0

    @pl.when(jnp.logical_not(any_tied))
    def _():
        def body(kb, carry):
            sl = pl.ds(pl.multiple_of(kb * tk, tk), tk)
            sel = jnp.logical_and(keys_ref[:, sl] >= thr, col0 + kb * tk < limit)
            bias_ref[:, sl] = jnp.where(sel, 0.0, NEG_INF)
            return carry
        lax.fori_loop(0, nkb, body, 0)

    @pl.when(any_tied)
    def _():
        need = top_k - count_ge(thr + 1)
        r_i = lax.broadcasted_iota(jnp.int32, (tk, tk), 0)
        c_i = lax.broadcasted_iota(jnp.int32, (tk, tk), 1)
        upper = jnp.where(r_i <= c_i, 1.0, 0.0).astype(BF16)

        def body(kb, seen):
            sl = pl.ds(pl.multiple_of(kb * tk, tk), tk)
            keys = keys_ref[:, sl]
            eq = keys == thr
            prefix = seen + jnp.dot(jnp.where(eq, 1.0, 0.0).astype(BF16), upper,
                                    preferred_element_type=F32)
            sel = jnp.logical_or(keys > thr, jnp.logical_and(eq, prefix <= need.astype(F32)))
            sel = jnp.logical_or(sel, jnp.logical_and(thr == KEY_NEG_INF, keys >= thr))
            sel = jnp.logical_and(sel, col0 + kb * tk < limit)
            bias_ref[:, sl] = jnp.where(sel, 0.0, NEG_INF)
            return prefix[:, tk - 1:tk]
        lax.fori_loop(0, nkb, body, jnp.zeros((tq, 1), F32))

    for h in range(DSA_HEADS):
        hs = slice(h * DSA_HEAD_DIM, (h + 1) * DSA_HEAD_DIM)
        qh = q_ref[:, hs]

        def attend(kb, carry, hs=hs, qh=qh):
            m, l, acc = carry
            sl = pl.ds(pl.multiple_of(kb * tk, tk), tk)
            s = _nt_dot(qh, k_ref[sl, hs]) + bias_ref[:, sl]
            m_new = jnp.maximum(m, jnp.max(s, axis=1, keepdims=True))
            m_use = jnp.where(m_new == NEG_INF, 0.0, m_new)
            p = jnp.exp(s - m_use)
            alpha = jnp.exp(m - m_use)
            l = alpha * l + jnp.sum(p, axis=1, keepdims=True)
            acc = alpha * acc + jnp.dot(p.astype(BF16), v_ref[sl, hs],
                                        preferred_element_type=F32)
            return m_new, l, acc

        init = (jnp.full((tq, 1), NEG_INF, F32), jnp.zeros((tq, 1), F32),
                jnp.zeros((tq, DSA_HEAD_DIM), F32))
        _, l, acc = lax.fori_loop(0, nkb, attend, init)
        y_ref[:, hs] = (acc / l).astype(y_ref.dtype)


def dsa_attention(q, k, v, qi, ki, w, B, S, tq, tk):
    T = B * S
    nq = S // tq
    top_k = min(TOPK_MAX, S // 4)
    qrow = lambda b, i: (b * nq + i, 0)
    seq = lambda b, i: (b, 0)
    return pl.pallas_call(
        functools.partial(_dsa_kernel, tq=tq, tk=tk, top_k=top_k),
        grid=(B, nq),
        in_specs=[
            pl.BlockSpec((tq, DSA_WIDTH), qrow),
            pl.BlockSpec((S, DSA_WIDTH), seq),
            pl.BlockSpec((S, DSA_WIDTH), seq),
            pl.BlockSpec((tq, IDX_HEADS * IDX_DIM), qrow),
            pl.BlockSpec((S, 2 * LANES), seq),
            pl.BlockSpec((tq, LANES), qrow),
        ],
        out_specs=pl.BlockSpec((tq, DSA_WIDTH), qrow),
        out_shape=jax.ShapeDtypeStruct((T, DSA_WIDTH), BF16),
        scratch_shapes=[pltpu.VMEM((tq, S), jnp.int32), pltpu.VMEM((tq, S), F32)],
        compiler_params=_cparams(("parallel", "arbitrary")),
        name="dsa_attention",
    )(q, k, v, qi, ki, w)


RW_PAIRS = RWKV_HEADS // 2
RW_GROUPS = RWKV_HEADS // 4


def _split3(x):
    hi = x.astype(BF16)
    r1 = x - hi.astype(F32)
    mid = r1.astype(BF16)
    lo = (r1 - mid.astype(F32)).astype(BF16)
    return hi, mid, lo


def _rwkv_kernel(p_ref, mu_ref, w0_ref, a0_ref, kk_ref, ka_ref, rk_ref, lnw_ref, lnb_ref,
                 wd_ref, wa_ref, wg_ref, y_ref, state_ref, prev_ref, *, C):
    c = pl.program_id(1)

    @pl.when(c == 0)
    def _():
        state_ref[...] = jnp.zeros_like(state_ref)
        prev_ref[...] = jnp.zeros_like(prev_ref)

    W = RWKV_WIDTH
    dot = functools.partial(jnp.dot, preferred_element_type=F32)

    p = p_ref[...]
    first_row = lax.broadcasted_iota(jnp.int32, p.shape, 0) == 0
    prev = jnp.where(first_row, prev_ref[...], pltpu.roll(p, 1, 0))
    prev_ref[...] = p[C - 1:C, :]
    xs = p + (prev - p) * mu_ref[...]

    small = xs[:, RWKV_SMALL:RWKV_SMALL + LANES]
    gate_in = xs[:, RWKV_SMALL + LANES:RWKV_SMALL + 3 * LANES]
    z = w0_ref[...] + dot(jnp.tanh(small).astype(BF16), wd_ref[...])
    softplus = jnp.maximum(-z, 0.0) + jnp.log(1.0 + jnp.exp(-jnp.abs(z)))
    logdec = -jnp.exp(-softplus - 0.5)
    rate = jax.nn.sigmoid(a0_ref[...] + dot(small.astype(BF16), wa_ref[...]))
    gate = dot(jax.nn.sigmoid(gate_in).astype(BF16), wg_ref[...])

    ti = lax.broadcasted_iota(jnp.int32, (C, C), 0)
    tj = lax.broadcasted_iota(jnp.int32, (C, C), 1)
    lower = jnp.where(tj <= ti, 1.0, 0.0).astype(BF16)
    hi, mid, lo = _split3(logdec)
    cum = dot(lower, hi) + dot(lower, mid) + dot(lower, lo)

    gi = lax.broadcasted_iota(jnp.int32, (LANES, LANES), 0) // RWKV_HEAD_DIM
    gj = lax.broadcasted_iota(jnp.int32, (LANES, LANES), 1) // RWKV_HEAD_DIM
    same_head = jnp.where(gi == gj, 1.0, 0.0).astype(BF16)

    def head_sum(parts):
        x = jnp.concatenate(parts, axis=0)
        xh = x.astype(BF16)
        xl = (x - xh.astype(F32)).astype(BF16)
        s = dot(xh, same_head) + dot(xl, same_head)
        return [s[n * C:(n + 1) * C] for n in range(len(parts))]

    lane = lax.broadcasted_iota(jnp.int32, (C, LANES), 1)
    even = lane < RWKV_HEAD_DIM
    zeros = jnp.zeros((C, LANES), F32)

    pairs = []
    for pi in range(RW_PAIRS):
        sl = slice(pi * LANES, (pi + 1) * LANES)
        r = xs[:, sl]
        k = xs[:, W + pi * LANES:W + (pi + 1) * LANES]
        v = xs[:, 2 * W + pi * LANES:2 * W + (pi + 1) * LANES]
        a = rate[:, sl]
        kk = k * kk_ref[:, sl]
        k2 = k * (1.0 + (a - 1.0) * ka_ref[:, sl])
        pairs.append(dict(r=r, v=v, a=a, kk=kk, k2=k2, sl=sl))
    sums = head_sum([d["kk"] * d["kk"] for d in pairs] + [d["r"] * d["k2"] * rk_ref[:, d["sl"]] for d in pairs])
    for pi, d in enumerate(pairs):
        sl = d["sl"]
        kkn = d["kk"] / jnp.maximum(jnp.sqrt(sums[pi]), 1e-12)
        d["bonus"] = sums[RW_PAIRS + pi] * d["v"]
        b = kkn * d["a"]
        cm = cum[:, sl]
        last = cm[C - 1:C, :]
        e_neg = jnp.exp(-cm)
        e_rem = jnp.exp(last - cm)
        d["At"] = -kkn * jnp.exp(cm - logdec[:, sl])
        d["Rt"] = d["r"] * jnp.exp(cm)
        d["Bt"] = b * e_neg
        d["Kt"] = d["k2"] * e_neg
        d["Bh"] = b * e_rem
        d["Kh"] = d["k2"] * e_rem
        d["wlast"] = jnp.exp(last)

    G = 4 * C
    ri = lax.broadcasted_iota(jnp.int32, (G, G), 0)
    ci = lax.broadcasted_iota(jnp.int32, (G, G), 1)
    same_block = (ri // C) == (ci // C)
    strict = jnp.logical_and(same_block, ci < ri)
    incl = jnp.logical_and(same_block, ci <= ri)
    pair_block = jnp.where(gi == gj, 1.0, 0.0)

    def masked4(x0, x1):
        return jnp.concatenate([
            jnp.concatenate([jnp.where(even, x0, 0.0), zeros], axis=1),
            jnp.concatenate([jnp.where(even, 0.0, x0), zeros], axis=1),
            jnp.concatenate([zeros, jnp.where(even, x1, 0.0)], axis=1),
            jnp.concatenate([zeros, jnp.where(even, 0.0, x1)], axis=1)], axis=0).astype(BF16)

    def plain4(x0, x1):
        return jnp.concatenate([
            jnp.concatenate([x0, zeros], axis=1), jnp.concatenate([x0, zeros], axis=1),
            jnp.concatenate([zeros, x1], axis=1), jnp.concatenate([zeros, x1], axis=1)],
            axis=0).astype(BF16)

    ys = []
    for g in range(RW_GROUPS):
        d0, d1 = pairs[2 * g], pairs[2 * g + 1]
        lhs_a = masked4(d0["At"], d1["At"])
        lhs_r = masked4(d0["Rt"], d1["Rt"])
        rhs_b = plain4(d0["Bt"], d1["Bt"])
        rhs_k = plain4(d0["Kt"], d1["Kt"])
        m_ab = jnp.where(strict, _nt_dot(lhs_a, rhs_b), 0.0).astype(BF16)
        m_ak = jnp.where(strict, _nt_dot(lhs_a, rhs_k), 0.0).astype(BF16)
        r_b = jnp.where(incl, _nt_dot(lhs_r, rhs_b), 0.0).astype(BF16)
        r_k = jnp.where(incl, _nt_dot(lhs_r, rhs_k), 0.0).astype(BF16)

        a0s, y0s = [], []
        for n, d in enumerate((d0, d1)):
            st = state_ref[2 * g + n].astype(BF16)
            both = _nt_dot(jnp.concatenate([d["At"], d["Rt"]], axis=0).astype(BF16), st)
            a0s.append(both[:C])
            y0s.append(both[C:])
        v4 = jnp.concatenate([d0["v"], d0["v"], d1["v"], d1["v"]], axis=0).astype(BF16)
        x = jnp.concatenate([a0s[0], a0s[0], a0s[1], a0s[1]], axis=0) + dot(m_ak, v4)
        m = m_ab
        steps = max(1, int(np.ceil(np.log2(C))))
        for s in range(steps):
            x = x + dot(m, x.astype(BF16))
            if s + 1 < steps:
                m = dot(m, m).astype(BF16)
        u4 = x.astype(BF16)
        yc = dot(r_b, u4) + dot(r_k, v4)
        for n, d in enumerate((d0, d1)):
            lo_, hi_ = 2 * n * C, (2 * n + 1) * C
            y = y0s[n] + jnp.where(even, yc[lo_:hi_], yc[hi_:hi_ + C])
            u = jnp.where(even, x[lo_:hi_], x[hi_:hi_ + C])
            uv = jnp.concatenate([u, d["v"]], axis=0).astype(BF16)
            bk = jnp.concatenate([d["Bh"], d["Kh"]], axis=0).astype(BF16)
            upd = lax.dot_general(uv, bk, (((0,), (0,)), ((), ())), preferred_element_type=F32)
            state_ref[2 * g + n] = state_ref[2 * g + n] * d["wlast"] + upd * pair_block
            ys.append(y)

    means = head_sum(ys)
    devs = [y - m_ * (1.0 / RWKV_HEAD_DIM) for y, m_ in zip(ys, means)]
    varis = head_sum([dv * dv for dv in devs])
    for pi, d in enumerate(pairs):
        sl = d["sl"]
        yn = devs[pi] * lax.rsqrt(varis[pi] * (1.0 / RWKV_HEAD_DIM) + GN_EPS)
        yn = yn * lnw_ref[:, sl] + lnb_ref[:, sl]
        y_ref[:, sl] = ((yn + d["bonus"]) * gate[:, sl]).astype(y_ref.dtype)


def rwkv_mixer(p_rwkv, mu, w0, a0, k_k, k_a, r_k, ln_w, ln_b, wd, wa, wg, B, S, C):
    T = B * S
    nc = S // C
    vec = pl.BlockSpec((1, RWKV_WIDTH), lambda b, c: (0, 0))
    return pl.pallas_call(
        functools.partial(_rwkv_kernel, C=C),
        grid=(B, nc),
        in_specs=[
            pl.BlockSpec((C, RWKV_PAD), lambda b, c: (b * nc + c, 0)),
            pl.BlockSpec((1, RWKV_PAD), lambda b, c: (0, 0)),
            vec, vec, vec, vec, vec, vec, vec,
            pl.BlockSpec((LANES, RWKV_WIDTH), lambda b, c: (0, 0)),
            pl.BlockSpec((LANES, RWKV_WIDTH), lambda b, c: (0, 0)),
            pl.BlockSpec((2 * LANES, RWKV_WIDTH), lambda b, c: (0, 0)),
        ],
        out_specs=pl.BlockSpec((C, RWKV_WIDTH), lambda b, c: (b * nc + c, 0)),
        out_shape=jax.ShapeDtypeStruct((T, RWKV_WIDTH), BF16),
        scratch_shapes=[pltpu.VMEM((RW_PAIRS, LANES, LANES), F32),
                        pltpu.VMEM((1, RWKV_PAD), F32)],
        compiler_params=_cparams(("parallel", "arbitrary")),
        name="rwkv_mixer",
    )(p_rwkv, mu, w0, a0, k_k, k_a, r_k, ln_w, ln_b, wd, wa, wg)


def _out_proj_kernel(x_ref, ya_ref, yb_ref, wa_ref, wb_ref, o_ref):
    o_ref[...] = (x_ref[...]
                  + jnp.dot(ya_ref[...], wa_ref[...], preferred_element_type=F32)
                  + jnp.dot(yb_ref[...], wb_ref[...], preferred_element_type=F32))


def out_proj(x, ya, yb, w_out, tm):
    T, D = x.shape
    half = ya.shape[1]
    row = lambda i: (i, 0)
    return pl.pallas_call(
        _out_proj_kernel,
        grid=(T // tm,),
        in_specs=[
            pl.BlockSpec((tm, D), row),
            pl.BlockSpec((tm, half), row),
            pl.BlockSpec((tm, half), row),
            pl.BlockSpec((half, D), lambda i: (0, 0)),
            pl.BlockSpec((half, D), lambda i: (1, 0)),
        ],
        out_specs=pl.BlockSpec((tm, D), row),
        out_shape=jax.ShapeDtypeStruct((T, D), F32),
        compiler_params=_cparams(("parallel",)),
        name="out_proj",
    )(x, ya, yb, w_out, w_out)


ROUTE_EXPERT0 = N_GROUPS


def _route(logits):
    lane = lax.broadcasted_iota(jnp.int32, logits.shape, 1)
    big = jnp.int32(LANES)
    is_grp = lane < N_GROUPS
    gl = jnp.where(is_grp, logits, NEG_INF)
    gmax = jnp.max(gl, axis=1, keepdims=True)
    p_grp = 1.0 / jnp.sum(jnp.exp(gl - gmax), axis=1, keepdims=True)
    g_sel = jnp.min(jnp.where(gl == gmax, lane, big), axis=1, keepdims=True)
    lo = ROUTE_EXPERT0 + g_sel * EXPERTS_PER_GROUP
    in_grp = jnp.logical_and(lane >= lo, lane < lo + EXPERTS_PER_GROUP)
    el = jnp.where(in_grp, logits, NEG_INF)
    ee = jnp.exp(el - jnp.max(el, axis=1, keepdims=True))
    pe = jnp.where(in_grp, ee / jnp.sum(ee, axis=1, keepdims=True), -1.0)
    p1 = jnp.max(pe, axis=1, keepdims=True)
    i1 = jnp.min(jnp.where(pe == p1, lane, big), axis=1, keepdims=True)
    pe2 = jnp.where(lane == i1, -1.0, pe)
    p2 = jnp.max(pe2, axis=1, keepdims=True)
    i2 = jnp.min(jnp.where(pe2 == p2, lane, big), axis=1, keepdims=True)
    tot = p1 + p2
    return jnp.where(lane == i1, p_grp * (p1 / tot), jnp.where(lane == i2, p_grp * (p2 / tot), 0.0))


def _moe_kernel(x_ref, g_ref, wr_hi_ref, wr_lo_ref, br_ref, wg_ref, wu_ref, wd_ref, o_ref,
                h_ref, comb_ref):
    e = pl.program_id(1)

    @pl.when(e == 0)
    def _():
        x = x_ref[...]
        ms = jnp.mean(x * x, axis=-1, keepdims=True)
        h = x * lax.rsqrt(ms + NORM_EPS) * g_ref[...]
        hi = h.astype(BF16)
        lo = (h - hi.astype(F32)).astype(BF16)
        h_ref[...] = hi
        dot = functools.partial(jnp.dot, preferred_element_type=F32)
        logits = dot(hi, wr_hi_ref[...]) + dot(lo, wr_hi_ref[...]) + dot(hi, wr_lo_ref[...])
        comb_ref[...] = _route(logits + br_ref[...])
        o_ref[...] = x

    h = h_ref[...]
    hg = jnp.dot(h, wg_ref[...], preferred_element_type=F32)
    hu = jnp.dot(h, wu_ref[...], preferred_element_type=F32)
    he = (hg * jax.nn.sigmoid(hg) * hu).astype(BF16)
    lane = lax.broadcasted_iota(jnp.int32, comb_ref.shape, 1)
    ce = jnp.sum(jnp.where(lane == e + ROUTE_EXPERT0, comb_ref[...], 0.0), axis=1, keepdims=True)
    o_ref[...] += ce * jnp.dot(he, wd_ref[...], preferred_element_type=F32)


def moe_block(x, g, wr_hi, wr_lo, br, w_gate, w_up, w_down, tm):
    T, D = x.shape
    E, _, Fd = w_gate.shape
    row = lambda i, e: (i, 0)
    fixed = lambda i, e: (0, 0)
    return pl.pallas_call(
        _moe_kernel,
        grid=(T // tm, E),
        in_specs=[
            pl.BlockSpec((tm, D), row),
            pl.BlockSpec((1, D), fixed),
            pl.BlockSpec((D, LANES), fixed),
            pl.BlockSpec((D, LANES), fixed),
            pl.BlockSpec((1, LANES), fixed),
            pl.BlockSpec((None, D, Fd), lambda i, e: (e, 0, 0)),
            pl.BlockSpec((None, D, Fd), lambda i, e: (e, 0, 0)),
            pl.BlockSpec((None, Fd, D), lambda i, e: (e, 0, 0)),
        ],
        out_specs=pl.BlockSpec((tm, D), row),
        out_shape=jax.ShapeDtypeStruct((T, D), F32),
        scratch_shapes=[pltpu.VMEM((tm, D), BF16), pltpu.VMEM((tm, LANES), F32)],
        compiler_params=_cparams(("parallel", "arbitrary")),
        name="moe_block",
    )(x, g, wr_hi, wr_lo, br, w_gate, w_up, w_down)


def _prep_route(w_route_group, b_route_group, w_route_expert, b_route_expert):
    D = w_route_group.shape[0]
    pad = LANES - N_GROUPS - N_EXPERTS
    wr = jnp.concatenate([w_route_group, w_route_expert, jnp.zeros((D, pad), F32)], axis=1)
    br = jnp.concatenate([b_route_group, b_route_expert, jnp.zeros((pad,), F32)])[None]
    hi = wr.astype(BF16)
    lo = (wr - hi.astype(F32)).astype(BF16)
    return hi, lo, br


def _prep_lora(w_decay_up, w_aicl_up, w_gate_lora_up):
    zw = jnp.zeros((LANES - DECAY_LORA, RWKV_WIDTH), w_decay_up.dtype)
    wd = jnp.concatenate([w_decay_up, zw], axis=0)
    wa = jnp.concatenate([jnp.zeros((DECAY_LORA, RWKV_WIDTH), w_aicl_up.dtype), w_aicl_up], axis=0)
    wg = jnp.concatenate([w_gate_lora_up,
                          jnp.zeros((2 * LANES - GATE_LORA, RWKV_WIDTH), w_gate_lora_up.dtype)], axis=0)
    return wd.astype(BF16), wa.astype(BF16), wg.astype(BF16)


DSA_COLS = 4176
RWKV_SPLITS = (RWKV_WIDTH, DECAY_LORA, RWKV_WIDTH, RWKV_WIDTH, AICL_LORA, GATE_LORA)


def _rwkv_cols(t):
    cuts = np.cumsum(RWKV_SPLITS)[:-1].tolist()
    r, dw, k, v, da, dg = jnp.split(t, cuts, axis=-1)
    pad = jnp.zeros(t.shape[:-1] + (RWKV_PAD - sum(RWKV_SPLITS),), t.dtype)
    return jnp.concatenate([r, k, v, dw, da, dg, pad], axis=-1)


def _prep_in_weights(w_in):
    pad = jnp.zeros((w_in.shape[0], DSA_PAD - DSA_COLS), w_in.dtype)
    w_dsa = jnp.concatenate([w_in[:, :DSA_COLS], pad], axis=1)
    return w_dsa.astype(BF16), _rwkv_cols(w_in[:, DSA_COLS:]).astype(BF16)


def kernel(x, g_mix, w_in, rwkv_shift_mix, q_gain, k_gain, w0, w_decay_up, a0, w_aicl_up, w_gate_lora_up, k_k, k_a, r_k, ln_x_w, ln_x_b, w_out, g_ffn, w_route_group, b_route_group, w_route_expert, b_route_expert, w_e_gate, w_e_up, w_e_down):
    B, S, D = x.shape
    T = B * S
    depth = g_mix.shape[0]
    xf = x.reshape(T, D)
    vec = lambda t: t.reshape(1, -1)
    for l in range(depth):
        w_dsa, w_rwkv = _prep_in_weights(w_in[l])
        p_dsa = norm_matmul(xf, vec(g_mix[l]), w_dsa, 1024, 256)
        p_rwkv = norm_matmul(xf, vec(g_mix[l]), w_rwkv, 1024, 256)
        q, k, v, qi, ki, w = dsa_prep(p_dsa, vec(q_gain[l]), vec(k_gain[l]), B, S, 256)
        y_dsa = dsa_attention(q, k, v, qi, ki, w, B, S, 128, 512)
        wd, wa, wg = _prep_lora(w_decay_up[l], w_aicl_up[l], w_gate_lora_up[l])
        y_rwkv = rwkv_mixer(p_rwkv, _rwkv_cols(vec(rwkv_shift_mix[l])), vec(w0[l]), vec(a0[l]),
                            vec(k_k[l]), vec(k_a[l]), vec(r_k[l]), vec(ln_x_w[l]), vec(ln_x_b[l]),
                            wd, wa, wg, B, S, CHUNK)
        xf = out_proj(xf, y_dsa, y_rwkv, w_out[l].astype(BF16), 512)
        wr_hi, wr_lo, br = _prep_route(w_route_group[l], b_route_group[l],
                                       w_route_expert[l], b_route_expert[l])
        xf = moe_block(xf, vec(g_ffn[l]), wr_hi, wr_lo, br, w_e_gate[l].astype(BF16),
                       w_e_up[l].astype(BF16), w_e_down[l].astype(BF16), 512)
    return xf.reshape(B, S, D)
```

```python
import functools

import jax
import jax.numpy as jnp
import numpy as np
from jax import lax
from jax.experimental import pallas as pl
from jax.experimental.pallas import tpu as pltpu

F32 = jnp.float32
BF16 = jnp.bfloat16

NORM_EPS = 1e-6
CHUNK = 64
DSA_HEADS = 8
DSA_HEAD_DIM = 128
DSA_WIDTH = DSA_HEADS * DSA_HEAD_DIM
IDX_HEADS = 16
IDX_DIM = 64
TOPK_MAX = 256
ROPE_THETA = 10000.0
RWKV_HEADS = 16
RWKV_HEAD_DIM = 64
RWKV_WIDTH = RWKV_HEADS * RWKV_HEAD_DIM
DECAY_LORA = 64
AICL_LORA = 64
GATE_LORA = 160
GN_EPS = 64e-5
N_GROUPS = 4
EXPERTS_PER_GROUP = 4
N_EXPERTS = 16
D_EXPERT = 512

LANES = 128
VMEM_LIMIT = 56 * 1024 * 1024

DSA_PAD = 4352
RWKV_PAD = 3584
DSA_SMALL = 4096
RWKV_SMALL = 3072

NEG_INF = float("-inf")
INT_MIN = -(2 ** 31)


def _cparams(sem):
    return pltpu.CompilerParams(dimension_semantics=sem, vmem_limit_bytes=VMEM_LIMIT)


def _norm_matmul_kernel(x_ref, g_ref, w_ref, o_ref, h_ref):
    @pl.when(pl.program_id(1) == 0)
    def _():
        x = x_ref[...]
        ms = jnp.mean(x * x, axis=-1, keepdims=True)
        h_ref[...] = (x * lax.rsqrt(ms + NORM_EPS) * g_ref[...]).astype(BF16)

    o_ref[...] = jnp.dot(h_ref[...], w_ref[...], preferred_element_type=F32)


def norm_matmul(x, g, w, tm, tn):
    T, D = x.shape
    N = w.shape[1]
    return pl.pallas_call(
        _norm_matmul_kernel,
        grid=(T // tm, N // tn),
        in_specs=[
            pl.BlockSpec((tm, D), lambda i, j: (i, 0)),
            pl.BlockSpec((1, D), lambda i, j: (0, 0)),
            pl.BlockSpec((D, tn), lambda i, j: (0, j)),
        ],
        out_specs=pl.BlockSpec((tm, tn), lambda i, j: (i, j)),
        out_shape=jax.ShapeDtypeStruct((T, N), F32),
        scratch_shapes=[pltpu.VMEM((tm, D), BF16)],
        compiler_params=_cparams(("parallel", "arbitrary")),
        name="norm_matmul",
    )(x, g, w)


def _rope_tables(S, d):
    half = d // 2
    inv = ROPE_THETA ** (-jnp.arange(half, dtype=F32) * (2.0 / d))
    ang = jnp.arange(S, dtype=F32)[:, None] * inv[None, :]
    cos, sin = jnp.cos(ang), jnp.sin(ang)
    return jnp.concatenate([cos, cos], axis=-1), jnp.concatenate([-sin, sin], axis=-1)


def _dsa_prep_kernel(p_ref, qg_ref, kg_ref, c128_ref, s128_ref, c64_ref, s64_ref,
                     qt_ref, k_ref, vt_ref, qit_ref, ki_ref, wt_ref):
    c128, s128 = c128_ref[...], s128_ref[...]
    c64, s64 = c64_ref[...], s64_ref[...]
    tm = c128.shape[0]
    first_half = (lax.broadcasted_iota(jnp.int32, (tm, LANES), 1) % IDX_DIM) < IDX_DIM // 2

    def rope128(y):
        return y * c128 + pltpu.roll(y, DSA_HEAD_DIM // 2, 1) * s128

    def rope64(y):
        partner = jnp.where(first_half, pltpu.roll(y, LANES - IDX_DIM // 2, 1),
                            pltpu.roll(y, IDX_DIM // 2, 1))
        return y * c64 + partner * s64

    def normed(x, gain):
        ms = jnp.mean(x * x, axis=-1, keepdims=True)
        return x * lax.rsqrt(ms + NORM_EPS) * gain

    scale = DSA_HEAD_DIM ** -0.5
    for h in range(DSA_HEADS):
        lo = h * DSA_HEAD_DIM
        q = p_ref[:, lo:lo + DSA_HEAD_DIM]
        qt_ref[lo:lo + DSA_HEAD_DIM, :] = (rope128(normed(q, qg_ref[...])) * scale).T.astype(BF16)
        k = p_ref[:, DSA_WIDTH + lo:DSA_WIDTH + lo + DSA_HEAD_DIM]
        k_ref[:, lo:lo + DSA_HEAD_DIM] = rope128(normed(k, kg_ref[...])).astype(BF16)
        v = p_ref[:, 2 * DSA_WIDTH + lo:2 * DSA_WIDTH + lo + DSA_HEAD_DIM]
        vt_ref[lo:lo + DSA_HEAD_DIM, :] = v.T.astype(BF16)
    for j in range(IDX_HEADS * IDX_DIM // LANES):
        lo = 3 * DSA_WIDTH + j * LANES
        qit_ref[j * LANES:(j + 1) * LANES, :] = rope64(p_ref[:, lo:lo + LANES]).T.astype(BF16)
    small = p_ref[:, DSA_SMALL:DSA_SMALL + LANES]
    lane = lax.broadcasted_iota(jnp.int32, (tm, LANES), 1)
    ki = rope64(small)
    ki_ref[:, :LANES] = jnp.where(lane < IDX_DIM, ki, 0.0).astype(BF16)
    ki_ref[:, LANES:] = jnp.where(lane >= IDX_DIM, pltpu.roll(ki, IDX_DIM, 1), 0.0).astype(BF16)
    wt_ref[...] = (small * (IDX_HEADS ** -0.5 * IDX_DIM ** -0.5)).T


def dsa_prep(p_dsa, q_gain, k_gain, B, S, tm):
    T = B * S
    c128, s128 = _rope_tables(S, DSA_HEAD_DIM)
    c64, s64 = _rope_tables(S, IDX_DIM)
    c64 = jnp.concatenate([c64, c64], axis=-1)
    s64 = jnp.concatenate([s64, s64], axis=-1)
    nsb = S // tm
    row = lambda i: (i, 0)
    pos = lambda i: (i % nsb, 0)
    fixed = lambda i: (0, 0)
    tile_t = lambda i: (i // nsb, i % nsb, 0, 0)
    wide_t = pl.BlockSpec((None, None, DSA_WIDTH, tm), tile_t)
    return pl.pallas_call(
        _dsa_prep_kernel,
        grid=(T // tm,),
        in_specs=[
            pl.BlockSpec((tm, DSA_PAD), row),
            pl.BlockSpec((1, DSA_HEAD_DIM), fixed),
            pl.BlockSpec((1, DSA_HEAD_DIM), fixed),
            pl.BlockSpec((tm, LANES), pos),
            pl.BlockSpec((tm, LANES), pos),
            pl.BlockSpec((tm, LANES), pos),
            pl.BlockSpec((tm, LANES), pos),
        ],
        out_specs=[
            wide_t,
            pl.BlockSpec((tm, DSA_WIDTH), row),
            wide_t,
            wide_t,
            pl.BlockSpec((tm, 2 * LANES), row),
            pl.BlockSpec((None, None, LANES, tm), tile_t),
        ],
        out_shape=[
            jax.ShapeDtypeStruct((B, nsb, DSA_WIDTH, tm), BF16),
            jax.ShapeDtypeStruct((T, DSA_WIDTH), BF16),
            jax.ShapeDtypeStruct((B, nsb, DSA_WIDTH, tm), BF16),
            jax.ShapeDtypeStruct((B, nsb, IDX_HEADS * IDX_DIM, tm), BF16),
            jax.ShapeDtypeStruct((T, 2 * LANES), BF16),
            jax.ShapeDtypeStruct((B, nsb, LANES, tm), F32),
        ],
        compiler_params=_cparams(("parallel",)),
        name="dsa_prep",
    )(p_dsa, q_gain, k_gain, c128, s128, c64, s64)


SCORE_ROWS = 128
SEARCH_CAP = 40
PEEL_FROM = 16
INF = float("inf")


def _nt_dot(a, b):
    return lax.dot_general(a, b, (((1,), (1,)), ((), ())), preferred_element_type=F32)


def _dsa_kernel(qt_ref, k_ref, vt_ref, qit_ref, ki_ref, wt_ref, y_ref,
                sc_ref, bias_ref, thr_ref, m_ref, l_ref, acc_ref, *, tq, tk, top_k):
    i = pl.program_id(1)
    nkb = ((i + 1) * tq + tk - 1) // tk
    qpos = i * tq + lax.broadcasted_iota(jnp.int32, (1, tq), 1)
    limit = (qpos // CHUNK + 1) * CHUNK
    kf = float(top_k)

    def blk(kb):
        return pl.ds(pl.multiple_of(kb * tk, tk), tk)

    def key_pos(kb, rows, r0=0):
        return kb * tk + r0 + lax.broadcasted_iota(jnp.int32, (rows, tq), 0)

    def score_block(kb, carry):
        mn, mx = carry
        for r0 in range(0, tk, SCORE_ROWS):
            kblk = ki_ref[pl.ds(pl.multiple_of(kb * tk, tk) + r0, SCORE_ROWS), :]
            acc = jnp.zeros((SCORE_ROWS, tq), F32)
            for j in range(IDX_HEADS // 2):
                qblk = qit_ref[j * LANES:(j + 1) * LANES, :]
                s_even = jnp.dot(kblk[:, :LANES], qblk, preferred_element_type=F32)
                s_odd = jnp.dot(kblk[:, LANES:], qblk, preferred_element_type=F32)
                acc = acc + jnp.maximum(s_even, 0.0) * wt_ref[IDX_DIM + 2 * j:IDX_DIM + 2 * j + 1, :]
                acc = acc + jnp.maximum(s_odd, 0.0) * wt_ref[IDX_DIM + 2 * j + 1:IDX_DIM + 2 * j + 2, :]
            adm = key_pos(kb, SCORE_ROWS, r0) < limit
            sc_ref[pl.ds(pl.multiple_of(kb * tk, tk) + r0, SCORE_ROWS), :] = jnp.where(adm, acc, NEG_INF)
            mn = jnp.minimum(mn, jnp.min(jnp.where(adm, acc, INF), axis=0, keepdims=True))
            mx = jnp.maximum(mx, jnp.max(jnp.where(adm, acc, NEG_INF), axis=0, keepdims=True))
        return mn, mx

    row_min, row_max = lax.fori_loop(
        0, nkb, score_block, (jnp.full((1, tq), INF, F32), jnp.full((1, tq), NEG_INF, F32)))

    def count_ge(x):
        def body(kb, acc):
            return acc + jnp.sum(jnp.where(sc_ref[blk(kb), :] >= x, 1.0, 0.0), axis=0, keepdims=True)
        return lax.fori_loop(0, nkb, body, jnp.zeros((1, tq), F32))

    def max_below(x):
        def body(kb, acc):
            s = sc_ref[blk(kb), :]
            return jnp.maximum(acc, jnp.max(jnp.where(s < x, s, NEG_INF), axis=0, keepdims=True))
        return lax.fori_loop(0, nkb, body, jnp.full((1, tq), NEG_INF, F32))

    take_all = limit <= top_k

    def search_cond(st):
        it, done = st[0], st[1]
        return jnp.logical_and(it < SEARCH_CAP, jnp.min(done) < 0.5)

    def search_step(st):
        it, done, thr, lo, hi = st
        mid = jnp.where(hi == INF, row_max, 0.5 * lo + 0.5 * hi)
        peel = jnp.logical_and(it >= PEEL_FROM, it % 4 == 3)
        mid = lax.cond(peel, lambda: max_below(hi), lambda: mid)
        c = count_ge(mid)
        active = done < 0.5
        hit = jnp.logical_or(c == kf, jnp.logical_and(peel, c >= kf))
        fin = jnp.logical_and(active, hit)
        up = jnp.logical_and(active, jnp.logical_and(jnp.logical_not(hit), c > kf))
        dn = jnp.logical_and(active, c < kf)
        return (it + 1, jnp.where(fin, 1.0, done), jnp.where(fin, mid, thr),
                jnp.where(up, mid, lo), jnp.where(dn, mid, hi))

    done0 = jnp.where(take_all, 1.0, 0.0)
    st = lax.while_loop(search_cond, search_step,
                        (jnp.int32(0), done0, jnp.full((1, tq), NEG_INF, F32), row_min,
                         jnp.full((1, tq), INF, F32)))
    done, thr = st[1], st[2]
    thr_ref[...] = thr

    @pl.when(jnp.min(done) < 0.5)
    def _():
        def from_key(key):
            return pltpu.bitcast(jnp.where(key < 0, key ^ jnp.int32(0x7FFFFFFF), key), F32)

        def bit_step(b, key):
            cand = key ^ jnp.left_shift(jnp.int32(1), 31 - b)
            return jnp.where(count_ge(from_key(cand)) >= kf, cand, key)

        key = lax.fori_loop(0, 32, bit_step, jnp.full((1, tq), INT_MIN, jnp.int32))
        thr_ref[...] = jnp.where(done < 0.5, from_key(key), thr)

    thr = thr_ref[...]
    tied = jnp.logical_and(count_ge(thr) > kf, jnp.logical_not(take_all))
    any_tied = jnp.max(jnp.where(tied, 1.0, 0.0)) > 0.5

    @pl.when(jnp.logical_not(any_tied))
    def _():
        def body(kb, carry):
            sel = jnp.logical_and(sc_ref[blk(kb), :] >= thr, key_pos(kb, tk) < limit)
            bias_ref[blk(kb), :] = jnp.where(sel, 0.0, NEG_INF)
            return carry
        lax.fori_loop(0, nkb, body, 0)

    @pl.when(any_tied)
    def _():
        def gt_body(kb, acc):
            return acc + jnp.sum(jnp.where(sc_ref[blk(kb), :] > thr, 1.0, 0.0), axis=0, keepdims=True)
        need = kf - lax.fori_loop(0, nkb, gt_body, jnp.zeros((1, tq), F32))
        r_i = lax.broadcasted_iota(jnp.int32, (tk, tk), 0)
        c_i = lax.broadcasted_iota(jnp.int32, (tk, tk), 1)
        lower = jnp.where(c_i <= r_i, 1.0, 0.0).astype(BF16)

        def body(kb, seen):
            s = sc_ref[blk(kb), :]
            eq = s == thr
            prefix = seen + jnp.dot(lower, jnp.where(eq, 1.0, 0.0).astype(BF16),
                                    preferred_element_type=F32)
            sel = jnp.logical_or(s > thr, jnp.logical_and(eq, prefix <= need))
            sel = jnp.logical_or(sel, take_all)
            sel = jnp.logical_and(sel, key_pos(kb, tk) < limit)
            bias_ref[blk(kb), :] = jnp.where(sel, 0.0, NEG_INF)
            return prefix[tk - 1:tk, :]
        lax.fori_loop(0, nkb, body, jnp.zeros((1, tq), F32))

    m_ref[...] = jnp.full(m_ref.shape, NEG_INF, F32)
    l_ref[...] = jnp.zeros(l_ref.shape, F32)
    acc_ref[...] = jnp.zeros(acc_ref.shape, F32)

    def attend(kb, carry):
        bias = bias_ref[blk(kb), :]
        for h in range(DSA_HEADS):
            hs = slice(h * DSA_HEAD_DIM, (h + 1) * DSA_HEAD_DIM)
            s = jnp.dot(k_ref[blk(kb), hs], qt_ref[hs, :], preferred_element_type=F32) + bias
            m = m_ref[h]
            m_new = jnp.maximum(m, jnp.max(s, axis=0, keepdims=True))
            m_use = jnp.where(m_new == NEG_INF, 0.0, m_new)
            p = jnp.exp(s - m_use)
            alpha = jnp.exp(m - m_use)
            m_ref[h] = m_new
            l_ref[h] = alpha * l_ref[h] + jnp.sum(p, axis=0, keepdims=True)
            acc_ref[h] = alpha * acc_ref[h] + jnp.dot(vt_ref[kb, hs, :], p.astype(BF16),
                                                      preferred_element_type=F32)
        return carry

    lax.fori_loop(0, nkb, attend, 0)
    for h in range(DSA_HEADS):
        hs = slice(h * DSA_HEAD_DIM, (h + 1) * DSA_HEAD_DIM)
        y_ref[:, hs] = (acc_ref[h] / l_ref[h]).T.astype(y_ref.dtype)


def dsa_attention(qt, k, vt, qit, ki, wt, B, S, tq, tk):
    assert tq == tk and qt.shape[-1] == tq
    T = B * S
    nq = S // tq
    top_k = min(TOPK_MAX, S // 4)
    qtile = lambda b, i: (b, i, 0, 0)
    seq = lambda b, i: (b, 0)
    return pl.pallas_call(
        functools.partial(_dsa_kernel, tq=tq, tk=tk, top_k=top_k),
        grid=(B, nq),
        in_specs=[
            pl.BlockSpec((None, None, DSA_WIDTH, tq), qtile),
            pl.BlockSpec((S, DSA_WIDTH), seq),
            pl.BlockSpec((None, nq, DSA_WIDTH, tk), lambda b, i: (b, 0, 0, 0)),
            pl.BlockSpec((None, None, IDX_HEADS * IDX_DIM, tq), qtile),
            pl.BlockSpec((S, 2 * LANES), seq),
            pl.BlockSpec((None, None, LANES, tq), qtile),
        ],
        out_specs=pl.BlockSpec((tq, DSA_WIDTH), lambda b, i: (b * nq + i, 0)),
        out_shape=jax.ShapeDtypeStruct((T, DSA_WIDTH), BF16),
        scratch_shapes=[pltpu.VMEM((S, tq), F32), pltpu.VMEM((S, tq), F32),
                        pltpu.VMEM((1, tq), F32),
                        pltpu.VMEM((DSA_HEADS, 1, tq), F32), pltpu.VMEM((DSA_HEADS, 1, tq), F32),
                        pltpu.VMEM((DSA_HEADS, DSA_HEAD_DIM, tq), F32)],
        compiler_params=_cparams(("parallel", "arbitrary")),
        name="dsa_attention",
    )(qt, k, vt, qit, ki, wt)


RW_PAIRS = RWKV_HEADS // 2
RW_GROUPS = RWKV_HEADS // 4


def _split3(x):
    hi = x.astype(BF16)
    r1 = x - hi.astype(F32)
    mid = r1.astype(BF16)
    lo = (r1 - mid.astype(F32)).astype(BF16)
    return hi, mid, lo


def _rwkv_kernel(p_ref, mu_ref, w0_ref, a0_ref, kk_ref, ka_ref, rk_ref, lnw_ref, lnb_ref,
                 wd_ref, wa_ref, wg_ref, y_ref, state_ref, prev_ref, *, C):
    c = pl.program_id(1)

    @pl.when(c == 0)
    def _():
        state_ref[...] = jnp.zeros_like(state_ref)
        prev_ref[...] = jnp.zeros_like(prev_ref)

    W = RWKV_WIDTH
    dot = functools.partial(jnp.dot, preferred_element_type=F32)

    p = p_ref[...]
    first_row = lax.broadcasted_iota(jnp.int32, p.shape, 0) == 0
    prev = jnp.where(first_row, prev_ref[...], pltpu.roll(p, 1, 0))
    prev_ref[...] = p[C - 1:C, :]
    xs = p + (prev - p) * mu_ref[...]

    small = xs[:, RWKV_SMALL:RWKV_SMALL + LANES]
    gate_in = xs[:, RWKV_SMALL + LANES:RWKV_SMALL + 3 * LANES]
    z = w0_ref[...] + dot(jnp.tanh(small).astype(BF16), wd_ref[...])
    softplus = jnp.maximum(-z, 0.0) + jnp.log(1.0 + jnp.exp(-jnp.abs(z)))
    logdec = -jnp.exp(-softplus - 0.5)
    rate = jax.nn.sigmoid(a0_ref[...] + dot(small.astype(BF16), wa_ref[...]))
    gate = dot(jax.nn.sigmoid(gate_in).astype(BF16), wg_ref[...])

    ti = lax.broadcasted_iota(jnp.int32, (C, C), 0)
    tj = lax.broadcasted_iota(jnp.int32, (C, C), 1)
    lower = jnp.where(tj <= ti, 1.0, 0.0).astype(BF16)
    hi, mid, lo = _split3(logdec)
    cum = dot(lower, hi) + dot(lower, mid) + dot(lower, lo)

    gi = lax.broadcasted_iota(jnp.int32, (LANES, LANES), 0) // RWKV_HEAD_DIM
    gj = lax.broadcasted_iota(jnp.int32, (LANES, LANES), 1) // RWKV_HEAD_DIM
    same_head = jnp.where(gi == gj, 1.0, 0.0).astype(BF16)

    def head_sum(parts):
        x = jnp.concatenate(parts, axis=0)
        xh = x.astype(BF16)
        xl = (x - xh.astype(F32)).astype(BF16)
        s = dot(xh, same_head) + dot(xl, same_head)
        return [s[n * C:(n + 1) * C] for n in range(len(parts))]

    lane = lax.broadcasted_iota(jnp.int32, (C, LANES), 1)
    even = lane < RWKV_HEAD_DIM
    zeros = jnp.zeros((C, LANES), F32)

    pairs = []
    for pi in range(RW_PAIRS):
        sl = slice(pi * LANES, (pi + 1) * LANES)
        r = xs[:, sl]
        k = xs[:, W + pi * LANES:W + (pi + 1) * LANES]
        v = xs[:, 2 * W + pi * LANES:2 * W + (pi + 1) * LANES]
        a = rate[:, sl]
        kk = k * kk_ref[:, sl]
        k2 = k * (1.0 + (a - 1.0) * ka_ref[:, sl])
        pairs.append(dict(r=r, v=v, a=a, kk=kk, k2=k2, sl=sl))
    sums = head_sum([d["kk"] * d["kk"] for d in pairs] + [d["r"] * d["k2"] * rk_ref[:, d["sl"]] for d in pairs])
    for pi, d in enumerate(pairs):
        sl = d["sl"]
        kkn = d["kk"] / jnp.maximum(jnp.sqrt(sums[pi]), 1e-12)
        d["bonus"] = sums[RW_PAIRS + pi] * d["v"]
        b = kkn * d["a"]
        cm = cum[:, sl]
        last = cm[C - 1:C, :]
        e_neg = jnp.exp(-cm)
        e_rem = jnp.exp(last - cm)
        d["At"] = -kkn * jnp.exp(cm - logdec[:, sl])
        d["Rt"] = d["r"] * jnp.exp(cm)
        d["Bt"] = b * e_neg
        d["Kt"] = d["k2"] * e_neg
        d["Bh"] = b * e_rem
        d["Kh"] = d["k2"] * e_rem
        d["wlast"] = jnp.exp(last)

    G = 4 * C
    ri = lax.broadcasted_iota(jnp.int32, (G, G), 0)
    ci = lax.broadcasted_iota(jnp.int32, (G, G), 1)
    same_block = (ri // C) == (ci // C)
    strict = jnp.logical_and(same_block, ci < ri)
    incl = jnp.logical_and(same_block, ci <= ri)
    pair_block = jnp.where(gi == gj, 1.0, 0.0)

    def masked4(x0, x1):
        return jnp.concatenate([
            jnp.concatenate([jnp.where(even, x0, 0.0), zeros], axis=1),
            jnp.concatenate([jnp.where(even, 0.0, x0), zeros], axis=1),
            jnp.concatenate([zeros, jnp.where(even, x1, 0.0)], axis=1),
            jnp.concatenate([zeros, jnp.where(even, 0.0, x1)], axis=1)], axis=0).astype(BF16)

    def plain4(x0, x1):
        return jnp.concatenate([
            jnp.concatenate([x0, zeros], axis=1), jnp.concatenate([x0, zeros], axis=1),
            jnp.concatenate([zeros, x1], axis=1), jnp.concatenate([zeros, x1], axis=1)],
            axis=0).astype(BF16)

    ys = []
    for g in range(RW_GROUPS):
        d0, d1 = pairs[2 * g], pairs[2 * g + 1]
        lhs_a = masked4(d0["At"], d1["At"])
        lhs_r = masked4(d0["Rt"], d1["Rt"])
        rhs_b = plain4(d0["Bt"], d1["Bt"])
        rhs_k = plain4(d0["Kt"], d1["Kt"])
        m_ab = jnp.where(strict, _nt_dot(lhs_a, rhs_b), 0.0).astype(BF16)
        m_ak = jnp.where(strict, _nt_dot(lhs_a, rhs_k), 0.0).astype(BF16)
        r_b = jnp.where(incl, _nt_dot(lhs_r, rhs_b), 0.0).astype(BF16)
        r_k = jnp.where(incl, _nt_dot(lhs_r, rhs_k), 0.0).astype(BF16)

        a0s, y0s = [], []
        for n, d in enumerate((d0, d1)):
            st = state_ref[2 * g + n].astype(BF16)
            both = _nt_dot(jnp.concatenate([d["At"], d["Rt"]], axis=0).astype(BF16), st)
            a0s.append(both[:C])
            y0s.append(both[C:])
        v4 = jnp.concatenate([d0["v"], d0["v"], d1["v"], d1["v"]], axis=0).astype(BF16)
        x = jnp.concatenate([a0s[0], a0s[0], a0s[1], a0s[1]], axis=0) + dot(m_ak, v4)
        m = m_ab
        steps = max(1, int(np.ceil(np.log2(C))))
        for s in range(steps):
            x = x + dot(m, x.astype(BF16))
            if s + 1 < steps:
                m = dot(m, m).astype(BF16)
        u4 = x.astype(BF16)
        yc = dot(r_b, u4) + dot(r_k, v4)
        for n, d in enumerate((d0, d1)):
            lo_, hi_ = 2 * n * C, (2 * n + 1) * C
            y = y0s[n] + jnp.where(even, yc[lo_:hi_], yc[hi_:hi_ + C])
            u = jnp.where(even, x[lo_:hi_], x[hi_:hi_ + C])
            uv = jnp.concatenate([u, d["v"]], axis=0).astype(BF16)
            bk = jnp.concatenate([d["Bh"], d["Kh"]], axis=0).astype(BF16)
            upd = lax.dot_general(uv, bk, (((0,), (0,)), ((), ())), preferred_element_type=F32)
            state_ref[2 * g + n] = state_ref[2 * g + n] * d["wlast"] + upd * pair_block
            ys.append(y)

    means = head_sum(ys)
    devs = [y - m_ * (1.0 / RWKV_HEAD_DIM) for y, m_ in zip(ys, means)]
    varis = head_sum([dv * dv for dv in devs])
    for pi, d in enumerate(pairs):
        sl = d["sl"]
        yn = devs[pi] * lax.rsqrt(varis[pi] * (1.0 / RWKV_HEAD_DIM) + GN_EPS)
        yn = yn * lnw_ref[:, sl] + lnb_ref[:, sl]
        y_ref[:, sl] = ((yn + d["bonus"]) * gate[:, sl]).astype(y_ref.dtype)


def rwkv_mixer(p_rwkv, mu, w0, a0, k_k, k_a, r_k, ln_w, ln_b, wd, wa, wg, B, S, C):
    T = B * S
    nc = S // C
    vec = pl.BlockSpec((1, RWKV_WIDTH), lambda b, c: (0, 0))
    return pl.pallas_call(
        functools.partial(_rwkv_kernel, C=C),
        grid=(B, nc),
        in_specs=[
            pl.BlockSpec((C, RWKV_PAD), lambda b, c: (b * nc + c, 0)),
            pl.BlockSpec((1, RWKV_PAD), lambda b, c: (0, 0)),
            vec, vec, vec, vec, vec, vec, vec,
            pl.BlockSpec((LANES, RWKV_WIDTH), lambda b, c: (0, 0)),
            pl.BlockSpec((LANES, RWKV_WIDTH), lambda b, c: (0, 0)),
            pl.BlockSpec((2 * LANES, RWKV_WIDTH), lambda b, c: (0, 0)),
        ],
        out_specs=pl.BlockSpec((C, RWKV_WIDTH), lambda b, c: (b * nc + c, 0)),
        out_shape=jax.ShapeDtypeStruct((T, RWKV_WIDTH), BF16),
        scratch_shapes=[pltpu.VMEM((RW_PAIRS, LANES, LANES), F32),
                        pltpu.VMEM((1, RWKV_PAD), F32)],
        compiler_params=_cparams(("parallel", "arbitrary")),
        name="rwkv_mixer",
    )(p_rwkv, mu, w0, a0, k_k, k_a, r_k, ln_w, ln_b, wd, wa, wg)


def _out_proj_kernel(x_ref, ya_ref, yb_ref, wa_ref, wb_ref, o_ref):
    o_ref[...] = (x_ref[...]
                  + jnp.dot(ya_ref[...], wa_ref[...], preferred_element_type=F32)
                  + jnp.dot(yb_ref[...], wb_ref[...], preferred_element_type=F32))


def out_proj(x, ya, yb, w_out, tm):
    T, D = x.shape
    half = ya.shape[1]
    row = lambda i: (i, 0)
    return pl.pallas_call(
        _out_proj_kernel,
        grid=(T // tm,),
        in_specs=[
            pl.BlockSpec((tm, D), row),
            pl.BlockSpec((tm, half), row),
            pl.BlockSpec((tm, half), row),
            pl.BlockSpec((half, D), lambda i: (0, 0)),
            pl.BlockSpec((half, D), lambda i: (1, 0)),
        ],
        out_specs=pl.BlockSpec((tm, D), row),
        out_shape=jax.ShapeDtypeStruct((T, D), F32),
        compiler_params=_cparams(("parallel",)),
        name="out_proj",
    )(x, ya, yb, w_out, w_out)


ROUTE_EXPERT0 = N_GROUPS


def _route(logits):
    lane = lax.broadcasted_iota(jnp.int32, logits.shape, 1)
    big = jnp.int32(LANES)
    is_grp = lane < N_GROUPS
    gl = jnp.where(is_grp, logits, NEG_INF)
    gmax = jnp.max(gl, axis=1, keepdims=True)
    p_grp = 1.0 / jnp.sum(jnp.exp(gl - gmax), axis=1, keepdims=True)
    g_sel = jnp.min(jnp.where(gl == gmax, lane, big), axis=1, keepdims=True)
    lo = ROUTE_EXPERT0 + g_sel * EXPERTS_PER_GROUP
    in_grp = jnp.logical_and(lane >= lo, lane < lo + EXPERTS_PER_GROUP)
    el = jnp.where(in_grp, logits, NEG_INF)
    ee = jnp.exp(el - jnp.max(el, axis=1, keepdims=True))
    pe = jnp.where(in_grp, ee / jnp.sum(ee, axis=1, keepdims=True), -1.0)
    p1 = jnp.max(pe, axis=1, keepdims=True)
    i1 = jnp.min(jnp.where(pe == p1, lane, big), axis=1, keepdims=True)
    pe2 = jnp.where(lane == i1, -1.0, pe)
    p2 = jnp.max(pe2, axis=1, keepdims=True)
    i2 = jnp.min(jnp.where(pe2 == p2, lane, big), axis=1, keepdims=True)
    tot = p1 + p2
    return jnp.where(lane == i1, p_grp * (p1 / tot), jnp.where(lane == i2, p_grp * (p2 / tot), 0.0))


def _moe_kernel(x_ref, g_ref, wr_hi_ref, wr_lo_ref, br_ref, wg_ref, wu_ref, wd_ref, o_ref,
                h_ref, comb_ref):
    e = pl.program_id(1)

    @pl.when(e == 0)
    def _():
        x = x_ref[...]
        ms = jnp.mean(x * x, axis=-1, keepdims=True)
        h = x * lax.rsqrt(ms + NORM_EPS) * g_ref[...]
        hi = h.astype(BF16)
        lo = (h - hi.astype(F32)).astype(BF16)
        h_ref[...] = hi
        dot = functools.partial(jnp.dot, preferred_element_type=F32)
        logits = dot(hi, wr_hi_ref[...]) + dot(lo, wr_hi_ref[...]) + dot(hi, wr_lo_ref[...])
        comb_ref[...] = _route(logits + br_ref[...])
        o_ref[...] = x

    h = h_ref[...]
    hg = jnp.dot(h, wg_ref[...], preferred_element_type=F32)
    hu = jnp.dot(h, wu_ref[...], preferred_element_type=F32)
    he = (hg * jax.nn.sigmoid(hg) * hu).astype(BF16)
    lane = lax.broadcasted_iota(jnp.int32, comb_ref.shape, 1)
    ce = jnp.sum(jnp.where(lane == e + ROUTE_EXPERT0, comb_ref[...], 0.0), axis=1, keepdims=True)
    o_ref[...] += ce * jnp.dot(he, wd_ref[...], preferred_element_type=F32)


def moe_block(x, g, wr_hi, wr_lo, br, w_gate, w_up, w_down, tm):
    T, D = x.shape
    E, _, Fd = w_gate.shape
    row = lambda i, e: (i, 0)
    fixed = lambda i, e: (0, 0)
    return pl.pallas_call(
        _moe_kernel,
        grid=(T // tm, E),
        in_specs=[
            pl.BlockSpec((tm, D), row),
            pl.BlockSpec((1, D), fixed),
            pl.BlockSpec((D, LANES), fixed),
            pl.BlockSpec((D, LANES), fixed),
            pl.BlockSpec((1, LANES), fixed),
            pl.BlockSpec((None, D, Fd), lambda i, e: (e, 0, 0)),
            pl.BlockSpec((None, D, Fd), lambda i, e: (e, 0, 0)),
            pl.BlockSpec((None, Fd, D), lambda i, e: (e, 0, 0)),
        ],
        out_specs=pl.BlockSpec((tm, D), row),
        out_shape=jax.ShapeDtypeStruct((T, D), F32),
        scratch_shapes=[pltpu.VMEM((tm, D), BF16), pltpu.VMEM((tm, LANES), F32)],
        compiler_params=_cparams(("parallel", "arbitrary")),
        name="moe_block",
    )(x, g, wr_hi, wr_lo, br, w_gate, w_up, w_down)


def _prep_route(w_route_group, b_route_group, w_route_expert, b_route_expert):
    D = w_route_group.shape[0]
    pad = LANES - N_GROUPS - N_EXPERTS
    wr = jnp.concatenate([w_route_group, w_route_expert, jnp.zeros((D, pad), F32)], axis=1)
    br = jnp.concatenate([b_route_group, b_route_expert, jnp.zeros((pad,), F32)])[None]
    hi = wr.astype(BF16)
    lo = (wr - hi.astype(F32)).astype(BF16)
    return hi, lo, br


def _prep_lora(w_decay_up, w_aicl_up, w_gate_lora_up):
    zw = jnp.zeros((LANES - DECAY_LORA, RWKV_WIDTH), w_decay_up.dtype)
    wd = jnp.concatenate([w_decay_up, zw], axis=0)
    wa = jnp.concatenate([jnp.zeros((DECAY_LORA, RWKV_WIDTH), w_aicl_up.dtype), w_aicl_up], axis=0)
    wg = jnp.concatenate([w_gate_lora_up,
                          jnp.zeros((2 * LANES - GATE_LORA, RWKV_WIDTH), w_gate_lora_up.dtype)], axis=0)
    return wd.astype(BF16), wa.astype(BF16), wg.astype(BF16)


DSA_COLS = 4176
RWKV_SPLITS = (RWKV_WIDTH, DECAY_LORA, RWKV_WIDTH, RWKV_WIDTH, AICL_LORA, GATE_LORA)


def _rwkv_cols(t):
    cuts = np.cumsum(RWKV_SPLITS)[:-1].tolist()
    r, dw, k, v, da, dg = jnp.split(t, cuts, axis=-1)
    pad = jnp.zeros(t.shape[:-1] + (RWKV_PAD - sum(RWKV_SPLITS),), t.dtype)
    return jnp.concatenate([r, k, v, dw, da, dg, pad], axis=-1)


def _prep_in_weights(w_in):
    pad = jnp.zeros((w_in.shape[0], DSA_PAD - DSA_COLS), w_in.dtype)
    w_dsa = jnp.concatenate([w_in[:, :DSA_COLS], pad], axis=1)
    return w_dsa.astype(BF16), _rwkv_cols(w_in[:, DSA_COLS:]).astype(BF16)


def kernel(x, g_mix, w_in, rwkv_shift_mix, q_gain, k_gain, w0, w_decay_up, a0, w_aicl_up, w_gate_lora_up, k_k, k_a, r_k, ln_x_w, ln_x_b, w_out, g_ffn, w_route_group, b_route_group, w_route_expert, b_route_expert, w_e_gate, w_e_up, w_e_down):
    B, S, D = x.shape
    T = B * S
    depth = g_mix.shape[0]
    xf = x.reshape(T, D)
    vec = lambda t: t.reshape(1, -1)
    for l in range(depth):
        w_dsa, w_rwkv = _prep_in_weights(w_in[l])
        p_dsa = norm_matmul(xf, vec(g_mix[l]), w_dsa, 1024, 256)
        p_rwkv = norm_matmul(xf, vec(g_mix[l]), w_rwkv, 1024, 256)
        q, k, v, qi, ki, w = dsa_prep(p_dsa, vec(q_gain[l]), vec(k_gain[l]), B, S, 256)
        y_dsa = dsa_attention(q, k, v, qi, ki, w, B, S, 256, 256)
        wd, wa, wg = _prep_lora(w_decay_up[l], w_aicl_up[l], w_gate_lora_up[l])
        y_rwkv = rwkv_mixer(p_rwkv, _rwkv_cols(vec(rwkv_shift_mix[l])), vec(w0[l]), vec(a0[l]),
                            vec(k_k[l]), vec(k_a[l]), vec(r_k[l]), vec(ln_x_w[l]), vec(ln_x_b[l]),
                            wd, wa, wg, B, S, CHUNK)
        xf = out_proj(xf, y_dsa, y_rwkv, w_out[l].astype(BF16), 512)
        wr_hi, wr_lo, br = _prep_route(w_route_group[l], b_route_group[l],
                                       w_route_expert[l], b_route_expert[l])
        xf = moe_block(xf, vec(g_ffn[l]), wr_hi, wr_lo, br, w_e_gate[l].astype(BF16),
                       w_e_up[l].astype(BF16), w_e_down[l].astype(BF16), 512)
    return xf.reshape(B, S, D)
```

```python
import functools

import jax
import jax.numpy as jnp
import numpy as np
from jax import lax
from jax.experimental import pallas as pl
from jax.experimental.pallas import tpu as pltpu

F32 = jnp.float32
BF16 = jnp.bfloat16

NORM_EPS = 1e-6
CHUNK = 64
DSA_HEADS = 8
DSA_HEAD_DIM = 128
DSA_WIDTH = DSA_HEADS * DSA_HEAD_DIM
IDX_HEADS = 16
IDX_DIM = 64
TOPK_MAX = 256
ROPE_THETA = 10000.0
RWKV_HEADS = 16
RWKV_HEAD_DIM = 64
RWKV_WIDTH = RWKV_HEADS * RWKV_HEAD_DIM
DECAY_LORA = 64
AICL_LORA = 64
GATE_LORA = 160
GN_EPS = 64e-5
N_GROUPS = 4
EXPERTS_PER_GROUP = 4
N_EXPERTS = 16
D_EXPERT = 512

LANES = 128
SUBLANES = 8
VMEM_LIMIT = 56 * 1024 * 1024

DSA_PAD = 4352
RWKV_PAD = 3584
DSA_SMALL = 4096
RWKV_SMALL = 3072

NEG_INF = float("-inf")
INT_MIN = -(2 ** 31)
LOG2E = 1.4426950408889634


def _cparams(sem):
    return pltpu.CompilerParams(dimension_semantics=sem, vmem_limit_bytes=VMEM_LIMIT)


def _norm_matmul_kernel(x_ref, g_ref, w_ref, o_ref, h_ref):
    @pl.when(pl.program_id(1) == 0)
    def _():
        x = x_ref[...]
        ms = jnp.mean(x * x, axis=-1, keepdims=True)
        h_ref[...] = (x * lax.rsqrt(ms + NORM_EPS) * g_ref[...]).astype(BF16)

    o_ref[...] = jnp.dot(h_ref[...], w_ref[...], preferred_element_type=F32)


def norm_matmul(x, g, w, tm, tn):
    T, D = x.shape
    N = w.shape[1]
    return pl.pallas_call(
        _norm_matmul_kernel,
        grid=(T // tm, N // tn),
        in_specs=[
            pl.BlockSpec((tm, D), lambda i, j: (i, 0)),
            pl.BlockSpec((1, D), lambda i, j: (0, 0)),
            pl.BlockSpec((D, tn), lambda i, j: (0, j)),
        ],
        out_specs=pl.BlockSpec((tm, tn), lambda i, j: (i, j)),
        out_shape=jax.ShapeDtypeStruct((T, N), F32),
        scratch_shapes=[pltpu.VMEM((tm, D), BF16)],
        compiler_params=_cparams(("parallel", "arbitrary")),
        name="norm_matmul",
    )(x, g, w)


def _rope_tables(S, d):
    half = d // 2
    inv = ROPE_THETA ** (-jnp.arange(half, dtype=F32) * (2.0 / d))
    ang = jnp.arange(S, dtype=F32)[:, None] * inv[None, :]
    cos, sin = jnp.cos(ang), jnp.sin(ang)
    return jnp.concatenate([cos, cos], axis=-1), jnp.concatenate([-sin, sin], axis=-1)


def _dsa_prep_kernel(p_ref, qg_ref, kg_ref, c128_ref, s128_ref, c64_ref, s64_ref,
                     qt_ref, k_ref, vt_ref, qit_ref, ki_ref, wt_ref):
    c128, s128 = c128_ref[...], s128_ref[...]
    c64, s64 = c64_ref[...], s64_ref[...]
    tm = c128.shape[0]
    first_half = (lax.broadcasted_iota(jnp.int32, (tm, LANES), 1) % IDX_DIM) < IDX_DIM // 2

    def rope128(y):
        return y * c128 + pltpu.roll(y, DSA_HEAD_DIM // 2, 1) * s128

    def rope64(y):
        partner = jnp.where(first_half, pltpu.roll(y, LANES - IDX_DIM // 2, 1),
                            pltpu.roll(y, IDX_DIM // 2, 1))
        return y * c64 + partner * s64

    def normed(x, gain):
        ms = jnp.mean(x * x, axis=-1, keepdims=True)
        return x * lax.rsqrt(ms + NORM_EPS) * gain

    scale = DSA_HEAD_DIM ** -0.5 * LOG2E
    for h in range(DSA_HEADS):
        lo = h * DSA_HEAD_DIM
        q = p_ref[:, lo:lo + DSA_HEAD_DIM]
        qt_ref[lo:lo + DSA_HEAD_DIM, :] = (rope128(normed(q, qg_ref[...])) * scale).T.astype(BF16)
        k = p_ref[:, DSA_WIDTH + lo:DSA_WIDTH + lo + DSA_HEAD_DIM]
        k_ref[:, lo:lo + DSA_HEAD_DIM] = rope128(normed(k, kg_ref[...])).astype(BF16)
        v = p_ref[:, 2 * DSA_WIDTH + lo:2 * DSA_WIDTH + lo + DSA_HEAD_DIM]
        vt_ref[lo:lo + DSA_HEAD_DIM, :] = v.T.astype(BF16)
    for j in range(IDX_HEADS * IDX_DIM // LANES):
        lo = 3 * DSA_WIDTH + j * LANES
        qit_ref[j * LANES:(j + 1) * LANES, :] = rope64(p_ref[:, lo:lo + LANES]).T.astype(BF16)
    small = p_ref[:, DSA_SMALL:DSA_SMALL + LANES]
    lane = lax.broadcasted_iota(jnp.int32, (tm, LANES), 1)
    ki = rope64(small)
    ki_ref[:, :LANES] = jnp.where(lane < IDX_DIM, ki, 0.0).astype(BF16)
    ki_ref[:, LANES:] = jnp.where(lane >= IDX_DIM, pltpu.roll(ki, IDX_DIM, 1), 0.0).astype(BF16)
    wt_ref[...] = (small * (IDX_HEADS ** -0.5 * IDX_DIM ** -0.5)).T


def dsa_prep(p_dsa, q_gain, k_gain, B, S, tm):
    T = B * S
    c128, s128 = _rope_tables(S, DSA_HEAD_DIM)
    c64, s64 = _rope_tables(S, IDX_DIM)
    c64 = jnp.concatenate([c64, c64], axis=-1)
    s64 = jnp.concatenate([s64, s64], axis=-1)
    nsb = S // tm
    row = lambda i: (i, 0)
    pos = lambda i: (i % nsb, 0)
    fixed = lambda i: (0, 0)
    tile_t = lambda i: (i // nsb, i % nsb, 0, 0)
    wide_t = pl.BlockSpec((None, None, DSA_WIDTH, tm), tile_t)
    return pl.pallas_call(
        _dsa_prep_kernel,
        grid=(T // tm,),
        in_specs=[
            pl.BlockSpec((tm, DSA_PAD), row),
            pl.BlockSpec((1, DSA_HEAD_DIM), fixed),
            pl.BlockSpec((1, DSA_HEAD_DIM), fixed),
            pl.BlockSpec((tm, LANES), pos),
            pl.BlockSpec((tm, LANES), pos),
            pl.BlockSpec((tm, LANES), pos),
            pl.BlockSpec((tm, LANES), pos),
        ],
        out_specs=[
            wide_t,
            pl.BlockSpec((tm, DSA_WIDTH), row),
            wide_t,
            wide_t,
            pl.BlockSpec((tm, 2 * LANES), row),
            pl.BlockSpec((None, None, LANES, tm), tile_t),
        ],
        out_shape=[
            jax.ShapeDtypeStruct((B, nsb, DSA_WIDTH, tm), BF16),
            jax.ShapeDtypeStruct((T, DSA_WIDTH), BF16),
            jax.ShapeDtypeStruct((B, nsb, DSA_WIDTH, tm), BF16),
            jax.ShapeDtypeStruct((B, nsb, IDX_HEADS * IDX_DIM, tm), BF16),
            jax.ShapeDtypeStruct((T, 2 * LANES), BF16),
            jax.ShapeDtypeStruct((B, nsb, LANES, tm), F32),
        ],
        compiler_params=_cparams(("parallel",)),
        name="dsa_prep",
    )(p_dsa, q_gain, k_gain, c128, s128, c64, s64)


SCORE_ROWS = 128
SEARCH_CAP = 40
PEEL_FROM = 16
INF = float("inf")
SHIFT_BOUND_MAX = 50.0
ONES_ROWS = 16


def _nt_dot(a, b):
    return lax.dot_general(a, b, (((1,), (1,)), ((), ())), preferred_element_type=F32)


def _dsa_kernel(qt_ref, k_ref, vt_ref, qit_ref, ki_ref, wt_ref, y_ref,
                sc_ref, bias_ref, thr_ref, m_ref, l_ref, acc_ref, ksq_ref,
                s0_ref, s1_ref, p0_ref, p1_ref, *, tq, tk, top_k):
    i = pl.program_id(1)
    nkb = ((i + 1) * tq + tk - 1) // tk
    qpos = i * tq + lax.broadcasted_iota(jnp.int32, (1, tq), 1)
    limit = (qpos // CHUNK + 1) * CHUNK
    kf = float(top_k)

    def blk(kb):
        return pl.ds(pl.multiple_of(kb * tk, tk), tk)

    def key_pos(kb, rows, r0=0):
        return kb * tk + r0 + lax.broadcasted_iota(jnp.int32, (rows, tq), 0)

    def score_block(kb, carry):
        mn, mx = carry
        for r0 in range(0, tk, SCORE_ROWS):
            kblk = ki_ref[pl.ds(pl.multiple_of(kb * tk, tk) + r0, SCORE_ROWS), :]
            acc = jnp.zeros((SCORE_ROWS, tq), F32)
            for j in range(IDX_HEADS // 2):
                qblk = qit_ref[j * LANES:(j + 1) * LANES, :]
                s_even = jnp.dot(kblk[:, :LANES], qblk, preferred_element_type=F32)
                s_odd = jnp.dot(kblk[:, LANES:], qblk, preferred_element_type=F32)
                acc = acc + jnp.maximum(s_even, 0.0) * wt_ref[IDX_DIM + 2 * j:IDX_DIM + 2 * j + 1, :]
                acc = acc + jnp.maximum(s_odd, 0.0) * wt_ref[IDX_DIM + 2 * j + 1:IDX_DIM + 2 * j + 2, :]
            adm = key_pos(kb, SCORE_ROWS, r0) < limit
            sc_ref[pl.ds(pl.multiple_of(kb * tk, tk) + r0, SCORE_ROWS), :] = jnp.where(adm, acc, NEG_INF)
            mn = jnp.minimum(mn, jnp.min(jnp.where(adm, acc, INF), axis=0, keepdims=True))
            mx = jnp.maximum(mx, jnp.max(jnp.where(adm, acc, NEG_INF), axis=0, keepdims=True))
        return mn, mx

    row_min, row_max = lax.fori_loop(
        0, nkb, score_block, (jnp.full((1, tq), INF, F32), jnp.full((1, tq), NEG_INF, F32)))

    def fold8(x, op):
        return op(x.reshape(x.shape[0] // SUBLANES, SUBLANES, tq), axis=0)

    def count_where(pred):
        def body(kb, acc):
            return acc + fold8(jnp.where(pred(sc_ref[blk(kb), :]), 1.0, 0.0), jnp.sum)
        part = lax.fori_loop(0, nkb, body, jnp.zeros((SUBLANES, tq), F32))
        return jnp.sum(part, axis=0, keepdims=True)

    def count_ge(x):
        return count_where(lambda s: s >= x)

    def max_below(x):
        def body(kb, acc):
            s = sc_ref[blk(kb), :]
            return jnp.maximum(acc, fold8(jnp.where(s < x, s, NEG_INF), jnp.max))
        part = lax.fori_loop(0, nkb, body, jnp.full((SUBLANES, tq), NEG_INF, F32))
        return jnp.max(part, axis=0, keepdims=True)

    take_all = limit <= top_k

    def search_cond(st):
        it, done = st[0], st[1]
        return jnp.logical_and(it < SEARCH_CAP, jnp.min(done) < 0.5)

    def search_step(st):
        it, done, thr, lo, hi = st
        mid = jnp.where(hi == INF, row_max, 0.5 * lo + 0.5 * hi)
        peel = jnp.logical_and(it >= PEEL_FROM, it % 4 == 3)
        mid = lax.cond(peel, lambda: max_below(hi), lambda: mid)
        c = count_ge(mid)
        active = done < 0.5
        hit = jnp.logical_or(c == kf, jnp.logical_and(peel, c >= kf))
        fin = jnp.logical_and(active, hit)
        up = jnp.logical_and(active, jnp.logical_and(jnp.logical_not(hit), c > kf))
        dn = jnp.logical_and(active, c < kf)
        return (it + 1, jnp.where(fin, 1.0, done), jnp.where(fin, mid, thr),
                jnp.where(up, mid, lo), jnp.where(dn, mid, hi))

    done0 = jnp.where(take_all, 1.0, 0.0)
    st = lax.while_loop(search_cond, search_step,
                        (jnp.int32(0), done0, jnp.full((1, tq), NEG_INF, F32), row_min,
                         jnp.full((1, tq), INF, F32)))
    done, thr = st[1], st[2]
    thr_ref[...] = thr

    @pl.when(jnp.min(done) < 0.5)
    def _():
        def from_key(key):
            return pltpu.bitcast(jnp.where(key < 0, key ^ jnp.int32(0x7FFFFFFF), key), F32)

        def bit_step(b, key):
            cand = key ^ jnp.left_shift(jnp.int32(1), 31 - b)
            return jnp.where(count_ge(from_key(cand)) >= kf, cand, key)

        key = lax.fori_loop(0, 32, bit_step, jnp.full((1, tq), INT_MIN, jnp.int32))
        thr_ref[...] = jnp.where(done < 0.5, from_key(key), thr)

    thr = thr_ref[...]
    tied = jnp.logical_and(count_ge(thr) > kf, jnp.logical_not(take_all))
    any_tied = jnp.max(jnp.where(tied, 1.0, 0.0)) > 0.5

    @pl.when(jnp.logical_not(any_tied))
    def _():
        def body(kb, carry):
            sel = jnp.logical_and(sc_ref[blk(kb), :] >= thr, key_pos(kb, tk) < limit)
            bias_ref[blk(kb), :] = jnp.where(sel, 0.0, NEG_INF)
            return carry
        lax.fori_loop(0, nkb, body, 0)

    @pl.when(any_tied)
    def _():
        need = kf - count_where(lambda s: s > thr)
        r_i = lax.broadcasted_iota(jnp.int32, (tk, tk), 0)
        c_i = lax.broadcasted_iota(jnp.int32, (tk, tk), 1)
        lower = jnp.where(c_i <= r_i, 1.0, 0.0).astype(BF16)

        def body(kb, seen):
            s = sc_ref[blk(kb), :]
            eq = s == thr
            prefix = seen + jnp.dot(lower, jnp.where(eq, 1.0, 0.0).astype(BF16),
                                    preferred_element_type=F32)
            sel = jnp.logical_or(s > thr, jnp.logical_and(eq, prefix <= need))
            sel = jnp.logical_or(sel, take_all)
            sel = jnp.logical_and(sel, key_pos(kb, tk) < limit)
            bias_ref[blk(kb), :] = jnp.where(sel, 0.0, NEG_INF)
            return prefix[tk - 1:tk, :]
        lax.fori_loop(0, nkb, body, jnp.zeros((1, tq), F32))

    heads = [slice(h * DSA_HEAD_DIM, (h + 1) * DSA_HEAD_DIM) for h in range(DSA_HEADS)]

    @pl.when(i == 0)
    def _():
        def body(kb, carry):
            out = []
            for h, hs in enumerate(heads):
                kh = k_ref[blk(kb), hs].astype(F32)
                sq = jnp.max(jnp.sum(kh * kh, axis=1, keepdims=True), axis=0, keepdims=True)
                out.append(jnp.maximum(carry[h], sq))
            return tuple(out)
        ksq = lax.fori_loop(0, k_ref.shape[0] // tk, body,
                            tuple(jnp.zeros((1, 1), F32) for _ in heads))
        for h in range(DSA_HEADS):
            ksq_ref[h] = jnp.broadcast_to(ksq[h], (1, tq))

    worst = jnp.zeros((1, tq), F32)
    for h, hs in enumerate(heads):
        qh = qt_ref[hs, :].astype(F32)
        bound = jnp.sqrt(jnp.sum(qh * qh, axis=0, keepdims=True) * ksq_ref[h])
        m_ref[h] = bound
        worst = jnp.maximum(worst, bound)
    shift_ok = jnp.max(worst) <= SHIFT_BOUND_MAX
    acc_ref[...] = jnp.zeros(acc_ref.shape, F32)

    @pl.when(shift_ok)
    def _():
        ones = jnp.ones((ONES_ROWS, tk), BF16)
        n_blocks = k_ref.shape[0] // tk
        bias_ref[pl.ds(n_blocks * tk, tk), :] = jnp.full((tk, tq), NEG_INF, F32)
        s1_ref[...] = jnp.zeros(s1_ref.shape, F32)
        p0_ref[...] = jnp.zeros(p0_ref.shape, BF16)

        def step(t, s_w, s_r, p_w, p_r):
            ta = jnp.minimum(t, nkb - 1)
            for h, hs in enumerate(heads):
                s_w[h] = jnp.dot(k_ref[blk(ta), hs], qt_ref[hs, :], preferred_element_type=F32)
            tb = t - 1
            valid = jnp.logical_and(tb >= 0, tb < nkb)
            bias = bias_ref[blk(jnp.where(valid, tb, n_blocks)), :]
            for h in range(DSA_HEADS):
                p_w[h] = jnp.exp2(s_r[h] + bias - m_ref[h]).astype(BF16)
            tc = jnp.maximum(t - 2, 0)
            for h, hs in enumerate(heads):
                acc_ref[h] += jnp.dot(jnp.concatenate([vt_ref[tc, hs, :], ones], axis=0), p_r[h],
                                      preferred_element_type=F32)

        def one_step(t, carry):
            @pl.when(t % 2 == 0)
            def _():
                step(t, s0_ref, s1_ref, p1_ref, p0_ref)

            @pl.when(t % 2 == 1)
            def _():
                step(t, s1_ref, s0_ref, p0_ref, p1_ref)
            return carry

        lax.fori_loop(0, nkb + 2, one_step, 0)
        for h, hs in enumerate(heads):
            acc = acc_ref[h]
            y = acc[:DSA_HEAD_DIM] / acc[DSA_HEAD_DIM:DSA_HEAD_DIM + 1]
            y_ref[:, hs] = y.T.astype(y_ref.dtype)

    @pl.when(jnp.logical_not(shift_ok))
    def _():
        m_ref[...] = jnp.full(m_ref.shape, NEG_INF, F32)
        l_ref[...] = jnp.zeros(l_ref.shape, F32)

        def attend(kb, carry):
            bias = bias_ref[blk(kb), :]
            for h, hs in enumerate(heads):
                s = jnp.dot(k_ref[blk(kb), hs], qt_ref[hs, :], preferred_element_type=F32) + bias
                m = m_ref[h]
                m_new = jnp.maximum(m, jnp.max(s, axis=0, keepdims=True))
                m_use = jnp.where(m_new == NEG_INF, 0.0, m_new)
                p = jnp.exp2(s - m_use)
                alpha = jnp.exp2(m - m_use)
                m_ref[h] = m_new
                l_ref[h] = alpha * l_ref[h] + jnp.sum(p, axis=0, keepdims=True)
                acc_ref[h, :DSA_HEAD_DIM, :] = (
                    alpha * acc_ref[h, :DSA_HEAD_DIM, :]
                    + jnp.dot(vt_ref[kb, hs, :], p.astype(BF16), preferred_element_type=F32))
            return carry

        lax.fori_loop(0, nkb, attend, 0)
        for h, hs in enumerate(heads):
            y_ref[:, hs] = (acc_ref[h, :DSA_HEAD_DIM, :] / l_ref[h]).T.astype(y_ref.dtype)


def dsa_attention(qt, k, vt, qit, ki, wt, B, S, tq, tk):
    assert tq == tk and qt.shape[-1] == tq
    T = B * S
    nq = S // tq
    top_k = min(TOPK_MAX, S // 4)
    qtile = lambda b, i: (b, i, 0, 0)
    seq = lambda b, i: (b, 0)
    return pl.pallas_call(
        functools.partial(_dsa_kernel, tq=tq, tk=tk, top_k=top_k),
        grid=(B, nq),
        in_specs=[
            pl.BlockSpec((None, None, DSA_WIDTH, tq), qtile),
            pl.BlockSpec((S, DSA_WIDTH), seq),
            pl.BlockSpec((None, nq, DSA_WIDTH, tk), lambda b, i: (b, 0, 0, 0)),
            pl.BlockSpec((None, None, IDX_HEADS * IDX_DIM, tq), qtile),
            pl.BlockSpec((S, 2 * LANES), seq),
            pl.BlockSpec((None, None, LANES, tq), qtile),
        ],
        out_specs=pl.BlockSpec((tq, DSA_WIDTH), lambda b, i: (b * nq + i, 0)),
        out_shape=jax.ShapeDtypeStruct((T, DSA_WIDTH), BF16),
        scratch_shapes=[pltpu.VMEM((S, tq), F32), pltpu.VMEM((S + tk, tq), F32),
                        pltpu.VMEM((1, tq), F32),
                        pltpu.VMEM((DSA_HEADS, 1, tq), F32), pltpu.VMEM((DSA_HEADS, 1, tq), F32),
                        pltpu.VMEM((DSA_HEADS, DSA_HEAD_DIM + ONES_ROWS, tq), F32),
                        pltpu.VMEM((DSA_HEADS, 1, tq), F32),
                        pltpu.VMEM((DSA_HEADS, tk, tq), F32), pltpu.VMEM((DSA_HEADS, tk, tq), F32),
                        pltpu.VMEM((DSA_HEADS, tk, tq), BF16), pltpu.VMEM((DSA_HEADS, tk, tq), BF16)],
        compiler_params=_cparams(("parallel", "arbitrary")),
        name="dsa_attention",
    )(qt, k, vt, qit, ki, wt)


RW_PAIRS = RWKV_HEADS // 2
RW_GROUPS = RWKV_HEADS // 4


def _split3(x):
    hi = x.astype(BF16)
    r1 = x - hi.astype(F32)
    mid = r1.astype(BF16)
    lo = (r1 - mid.astype(F32)).astype(BF16)
    return hi, mid, lo


def _rwkv_kernel(p_ref, mu_ref, w0_ref, a0_ref, kk_ref, ka_ref, rk_ref, lnw_ref, lnb_ref,
                 wd_ref, wa_ref, wg_ref, y_ref, state_ref, prev_ref, *, C):
    c = pl.program_id(1)

    @pl.when(c == 0)
    def _():
        state_ref[...] = jnp.zeros_like(state_ref)
        prev_ref[...] = jnp.zeros_like(prev_ref)

    W = RWKV_WIDTH
    dot = functools.partial(jnp.dot, preferred_element_type=F32)

    p = p_ref[...]
    first_row = lax.broadcasted_iota(jnp.int32, p.shape, 0) == 0
    prev = jnp.where(first_row, prev_ref[...], pltpu.roll(p, 1, 0))
    prev_ref[...] = p[C - 1:C, :]
    xs = p + (prev - p) * mu_ref[...]

    small = xs[:, RWKV_SMALL:RWKV_SMALL + LANES]
    gate_in = xs[:, RWKV_SMALL + LANES:RWKV_SMALL + 3 * LANES]
    z = w0_ref[...] + dot(jnp.tanh(small).astype(BF16), wd_ref[...])
    softplus = jnp.maximum(-z, 0.0) + jnp.log(1.0 + jnp.exp(-jnp.abs(z)))
    logdec = -jnp.exp(-softplus - 0.5)
    rate = jax.nn.sigmoid(a0_ref[...] + dot(small.astype(BF16), wa_ref[...]))
    gate = dot(jax.nn.sigmoid(gate_in).astype(BF16), wg_ref[...])

    ti = lax.broadcasted_iota(jnp.int32, (C, C), 0)
    tj = lax.broadcasted_iota(jnp.int32, (C, C), 1)
    lower = jnp.where(tj <= ti, 1.0, 0.0).astype(BF16)
    hi, mid, lo = _split3(logdec)
    cum = dot(lower, hi) + dot(lower, mid) + dot(lower, lo)

    gi = lax.broadcasted_iota(jnp.int32, (LANES, LANES), 0) // RWKV_HEAD_DIM
    gj = lax.broadcasted_iota(jnp.int32, (LANES, LANES), 1) // RWKV_HEAD_DIM
    same_head = jnp.where(gi == gj, 1.0, 0.0).astype(BF16)

    def head_sum(parts):
        x = jnp.concatenate(parts, axis=0)
        xh = x.astype(BF16)
        xl = (x - xh.astype(F32)).astype(BF16)
        s = dot(xh, same_head) + dot(xl, same_head)
        return [s[n * C:(n + 1) * C] for n in range(len(parts))]

    lane = lax.broadcasted_iota(jnp.int32, (C, LANES), 1)
    even = lane < RWKV_HEAD_DIM
    zeros = jnp.zeros((C, LANES), F32)

    pairs = []
    for pi in range(RW_PAIRS):
        sl = slice(pi * LANES, (pi + 1) * LANES)
        r = xs[:, sl]
        k = xs[:, W + pi * LANES:W + (pi + 1) * LANES]
        v = xs[:, 2 * W + pi * LANES:2 * W + (pi + 1) * LANES]
        a = rate[:, sl]
        kk = k * kk_ref[:, sl]
        k2 = k * (1.0 + (a - 1.0) * ka_ref[:, sl])
        pairs.append(dict(r=r, v=v, a=a, kk=kk, k2=k2, sl=sl))
    sums = head_sum([d["kk"] * d["kk"] for d in pairs] + [d["r"] * d["k2"] * rk_ref[:, d["sl"]] for d in pairs])
    for pi, d in enumerate(pairs):
        sl = d["sl"]
        kkn = d["kk"] / jnp.maximum(jnp.sqrt(sums[pi]), 1e-12)
        d["bonus"] = sums[RW_PAIRS + pi] * d["v"]
        b = kkn * d["a"]
        cm = cum[:, sl]
        last = cm[C - 1:C, :]
        e_neg = jnp.exp(-cm)
        e_rem = jnp.exp(last - cm)
        d["At"] = -kkn * jnp.exp(cm - logdec[:, sl])
        d["Rt"] = d["r"] * jnp.exp(cm)
        d["Bt"] = b * e_neg
        d["Kt"] = d["k2"] * e_neg
        d["Bh"] = b * e_rem
        d["Kh"] = d["k2"] * e_rem
        d["wlast"] = jnp.exp(last)

    G = 4 * C
    ri = lax.broadcasted_iota(jnp.int32, (G, G), 0)
    ci = lax.broadcasted_iota(jnp.int32, (G, G), 1)
    same_block = (ri // C) == (ci // C)
    strict = jnp.logical_and(same_block, ci < ri)
    incl = jnp.logical_and(same_block, ci <= ri)
    pair_block = jnp.where(gi == gj, 1.0, 0.0)

    def masked4(x0, x1):
        return jnp.concatenate([
            jnp.concatenate([jnp.where(even, x0, 0.0), zeros], axis=1),
            jnp.concatenate([jnp.where(even, 0.0, x0), zeros], axis=1),
            jnp.concatenate([zeros, jnp.where(even, x1, 0.0)], axis=1),
            jnp.concatenate([zeros, jnp.where(even, 0.0, x1)], axis=1)], axis=0).astype(BF16)

    def plain4(x0, x1):
        return jnp.concatenate([
            jnp.concatenate([x0, zeros], axis=1), jnp.concatenate([x0, zeros], axis=1),
            jnp.concatenate([zeros, x1], axis=1), jnp.concatenate([zeros, x1], axis=1)],
            axis=0).astype(BF16)

    groups = []
    for g in range(RW_GROUPS):
        d0, d1 = pairs[2 * g], pairs[2 * g + 1]
        lhs_a = masked4(d0["At"], d1["At"])
        lhs_r = masked4(d0["Rt"], d1["Rt"])
        rhs_b = plain4(d0["Bt"], d1["Bt"])
        rhs_k = plain4(d0["Kt"], d1["Kt"])
        groups.append(dict(
            d=(d0, d1),
            m=jnp.where(strict, _nt_dot(lhs_a, rhs_b), 0.0).astype(BF16),
            m_ak=jnp.where(strict, _nt_dot(lhs_a, rhs_k), 0.0).astype(BF16),
            r_b=jnp.where(incl, _nt_dot(lhs_r, rhs_b), 0.0).astype(BF16),
            r_k=jnp.where(incl, _nt_dot(lhs_r, rhs_k), 0.0).astype(BF16),
            v4=jnp.concatenate([d0["v"], d0["v"], d1["v"], d1["v"]], axis=0).astype(BF16)))
    for g, grp in enumerate(groups):
        a0s, y0s = [], []
        for n, d in enumerate(grp["d"]):
            st = state_ref[2 * g + n].astype(BF16)
            both = _nt_dot(jnp.concatenate([d["At"], d["Rt"]], axis=0).astype(BF16), st)
            a0s.append(both[:C])
            y0s.append(both[C:])
        grp["y0"] = y0s
        grp["x"] = (jnp.concatenate([a0s[0], a0s[0], a0s[1], a0s[1]], axis=0)
                    + dot(grp["m_ak"], grp["v4"]))
    steps = max(1, int(np.ceil(np.log2(C))))
    for s in range(steps):
        for grp in groups:
            grp["x"] = grp["x"] + dot(grp["m"], grp["x"].astype(BF16))
        if s + 1 < steps:
            for grp in groups:
                grp["m"] = dot(grp["m"], grp["m"]).astype(BF16)
    ys = []
    for g, grp in enumerate(groups):
        x = grp["x"]
        yc = dot(grp["r_b"], x.astype(BF16)) + dot(grp["r_k"], grp["v4"])
        for n, d in enumerate(grp["d"]):
            lo_, hi_ = 2 * n * C, (2 * n + 1) * C
            y = grp["y0"][n] + jnp.where(even, yc[lo_:hi_], yc[hi_:hi_ + C])
            u = jnp.where(even, x[lo_:hi_], x[hi_:hi_ + C])
            uv = jnp.concatenate([u, d["v"]], axis=0).astype(BF16)
            bk = jnp.concatenate([d["Bh"], d["Kh"]], axis=0).astype(BF16)
            upd = lax.dot_general(uv, bk, (((0,), (0,)), ((), ())), preferred_element_type=F32)
            state_ref[2 * g + n] = state_ref[2 * g + n] * d["wlast"] + upd * pair_block
            ys.append(y)

    means = head_sum(ys)
    devs = [y - m_ * (1.0 / RWKV_HEAD_DIM) for y, m_ in zip(ys, means)]
    varis = head_sum([dv * dv for dv in devs])
    for pi, d in enumerate(pairs):
        sl = d["sl"]
        yn = devs[pi] * lax.rsqrt(varis[pi] * (1.0 / RWKV_HEAD_DIM) + GN_EPS)
        yn = yn * lnw_ref[:, sl] + lnb_ref[:, sl]
        y_ref[:, sl] = ((yn + d["bonus"]) * gate[:, sl]).astype(y_ref.dtype)


def rwkv_mixer(p_rwkv, mu, w0, a0, k_k, k_a, r_k, ln_w, ln_b, wd, wa, wg, B, S, C):
    T = B * S
    nc = S // C
    vec = pl.BlockSpec((1, RWKV_WIDTH), lambda b, c: (0, 0))
    return pl.pallas_call(
        functools.partial(_rwkv_kernel, C=C),
        grid=(B, nc),
        in_specs=[
            pl.BlockSpec((C, RWKV_PAD), lambda b, c: (b * nc + c, 0)),
            pl.BlockSpec((1, RWKV_PAD), lambda b, c: (0, 0)),
            vec, vec, vec, vec, vec, vec, vec,
            pl.BlockSpec((LANES, RWKV_WIDTH), lambda b, c: (0, 0)),
            pl.BlockSpec((LANES, RWKV_WIDTH), lambda b, c: (0, 0)),
            pl.BlockSpec((2 * LANES, RWKV_WIDTH), lambda b, c: (0, 0)),
        ],
        out_specs=pl.BlockSpec((C, RWKV_WIDTH), lambda b, c: (b * nc + c, 0)),
        out_shape=jax.ShapeDtypeStruct((T, RWKV_WIDTH), BF16),
        scratch_shapes=[pltpu.VMEM((RW_PAIRS, LANES, LANES), F32),
                        pltpu.VMEM((1, RWKV_PAD), F32)],
        compiler_params=_cparams(("parallel", "arbitrary")),
        name="rwkv_mixer",
    )(p_rwkv, mu, w0, a0, k_k, k_a, r_k, ln_w, ln_b, wd, wa, wg)


def _out_proj_kernel(x_ref, ya_ref, yb_ref, wa_ref, wb_ref, o_ref):
    o_ref[...] = (x_ref[...]
                  + jnp.dot(ya_ref[...], wa_ref[...], preferred_element_type=F32)
                  + jnp.dot(yb_ref[...], wb_ref[...], preferred_element_type=F32))


def out_proj(x, ya, yb, w_out, tm):
    T, D = x.shape
    half = ya.shape[1]
    row = lambda i: (i, 0)
    return pl.pallas_call(
        _out_proj_kernel,
        grid=(T // tm,),
        in_specs=[
            pl.BlockSpec((tm, D), row),
            pl.BlockSpec((tm, half), row),
            pl.BlockSpec((tm, half), row),
            pl.BlockSpec((half, D), lambda i: (0, 0)),
            pl.BlockSpec((half, D), lambda i: (1, 0)),
        ],
        out_specs=pl.BlockSpec((tm, D), row),
        out_shape=jax.ShapeDtypeStruct((T, D), F32),
        compiler_params=_cparams(("parallel",)),
        name="out_proj",
    )(x, ya, yb, w_out, w_out)


ROUTE_EXPERT0 = N_GROUPS


def _route(logits):
    lane = lax.broadcasted_iota(jnp.int32, logits.shape, 1)
    big = jnp.int32(LANES)
    is_grp = lane < N_GROUPS
    gl = jnp.where(is_grp, logits, NEG_INF)
    gmax = jnp.max(gl, axis=1, keepdims=True)
    p_grp = 1.0 / jnp.sum(jnp.exp(gl - gmax), axis=1, keepdims=True)
    g_sel = jnp.min(jnp.where(gl == gmax, lane, big), axis=1, keepdims=True)
    lo = ROUTE_EXPERT0 + g_sel * EXPERTS_PER_GROUP
    in_grp = jnp.logical_and(lane >= lo, lane < lo + EXPERTS_PER_GROUP)
    el = jnp.where(in_grp, logits, NEG_INF)
    ee = jnp.exp(el - jnp.max(el, axis=1, keepdims=True))
    pe = jnp.where(in_grp, ee / jnp.sum(ee, axis=1, keepdims=True), -1.0)
    p1 = jnp.max(pe, axis=1, keepdims=True)
    i1 = jnp.min(jnp.where(pe == p1, lane, big), axis=1, keepdims=True)
    pe2 = jnp.where(lane == i1, -1.0, pe)
    p2 = jnp.max(pe2, axis=1, keepdims=True)
    i2 = jnp.min(jnp.where(pe2 == p2, lane, big), axis=1, keepdims=True)
    tot = p1 + p2
    return jnp.where(lane == i1, p_grp * (p1 / tot), jnp.where(lane == i2, p_grp * (p2 / tot), 0.0))


def _moe_kernel(x_ref, g_ref, wr_hi_ref, wr_lo_ref, br_ref, wg_ref, wu_ref, wd_ref, o_ref,
                h_ref, comb_ref):
    e = pl.program_id(1)

    @pl.when(e == 0)
    def _():
        x = x_ref[...]
        ms = jnp.mean(x * x, axis=-1, keepdims=True)
        h = x * lax.rsqrt(ms + NORM_EPS) * g_ref[...]
        hi = h.astype(BF16)
        lo = (h - hi.astype(F32)).astype(BF16)
        h_ref[...] = hi
        dot = functools.partial(jnp.dot, preferred_element_type=F32)
        logits = dot(hi, wr_hi_ref[...]) + dot(lo, wr_hi_ref[...]) + dot(hi, wr_lo_ref[...])
        comb_ref[...] = _route(logits + br_ref[...])
        o_ref[...] = x

    h = h_ref[...]
    hg = jnp.dot(h, wg_ref[...], preferred_element_type=F32)
    hu = jnp.dot(h, wu_ref[...], preferred_element_type=F32)
    he = (hg * jax.nn.sigmoid(hg) * hu).astype(BF16)
    lane = lax.broadcasted_iota(jnp.int32, comb_ref.shape, 1)
    ce = jnp.sum(jnp.where(lane == e + ROUTE_EXPERT0, comb_ref[...], 0.0), axis=1, keepdims=True)
    o_ref[...] += ce * jnp.dot(he, wd_ref[...], preferred_element_type=F32)


def moe_block(x, g, wr_hi, wr_lo, br, w_gate, w_up, w_down, tm):
    T, D = x.shape
    E, _, Fd = w_gate.shape
    row = lambda i, e: (i, 0)
    fixed = lambda i, e: (0, 0)
    return pl.pallas_call(
        _moe_kernel,
        grid=(T // tm, E),
        in_specs=[
            pl.BlockSpec((tm, D), row),
            pl.BlockSpec((1, D), fixed),
            pl.BlockSpec((D, LANES), fixed),
            pl.BlockSpec((D, LANES), fixed),
            pl.BlockSpec((1, LANES), fixed),
            pl.BlockSpec((None, D, Fd), lambda i, e: (e, 0, 0)),
            pl.BlockSpec((None, D, Fd), lambda i, e: (e, 0, 0)),
            pl.BlockSpec((None, Fd, D), lambda i, e: (e, 0, 0)),
        ],
        out_specs=pl.BlockSpec((tm, D), row),
        out_shape=jax.ShapeDtypeStruct((T, D), F32),
        scratch_shapes=[pltpu.VMEM((tm, D), BF16), pltpu.VMEM((tm, LANES), F32)],
        compiler_params=_cparams(("parallel", "arbitrary")),
        name="moe_block",
    )(x, g, wr_hi, wr_lo, br, w_gate, w_up, w_down)


def _prep_route(w_route_group, b_route_group, w_route_expert, b_route_expert):
    D = w_route_group.shape[0]
    pad = LANES - N_GROUPS - N_EXPERTS
    wr = jnp.concatenate([w_route_group, w_route_expert, jnp.zeros((D, pad), F32)], axis=1)
    br = jnp.concatenate([b_route_group, b_route_expert, jnp.zeros((pad,), F32)])[None]
    hi = wr.astype(BF16)
    lo = (wr - hi.astype(F32)).astype(BF16)
    return hi, lo, br


def _prep_lora(w_decay_up, w_aicl_up, w_gate_lora_up):
    zw = jnp.zeros((LANES - DECAY_LORA, RWKV_WIDTH), w_decay_up.dtype)
    wd = jnp.concatenate([w_decay_up, zw], axis=0)
    wa = jnp.concatenate([jnp.zeros((DECAY_LORA, RWKV_WIDTH), w_aicl_up.dtype), w_aicl_up], axis=0)
    wg = jnp.concatenate([w_gate_lora_up,
                          jnp.zeros((2 * LANES - GATE_LORA, RWKV_WIDTH), w_gate_lora_up.dtype)], axis=0)
    return wd.astype(BF16), wa.astype(BF16), wg.astype(BF16)


DSA_COLS = 4176
RWKV_SPLITS = (RWKV_WIDTH, DECAY_LORA, RWKV_WIDTH, RWKV_WIDTH, AICL_LORA, GATE_LORA)


def _rwkv_cols(t):
    cuts = np.cumsum(RWKV_SPLITS)[:-1].tolist()
    r, dw, k, v, da, dg = jnp.split(t, cuts, axis=-1)
    pad = jnp.zeros(t.shape[:-1] + (RWKV_PAD - sum(RWKV_SPLITS),), t.dtype)
    return jnp.concatenate([r, k, v, dw, da, dg, pad], axis=-1)


def _prep_in_weights(w_in):
    pad = jnp.zeros((w_in.shape[0], DSA_PAD - DSA_COLS), w_in.dtype)
    w_dsa = jnp.concatenate([w_in[:, :DSA_COLS], pad], axis=1)
    return w_dsa.astype(BF16), _rwkv_cols(w_in[:, DSA_COLS:]).astype(BF16)


def kernel(x, g_mix, w_in, rwkv_shift_mix, q_gain, k_gain, w0, w_decay_up, a0, w_aicl_up, w_gate_lora_up, k_k, k_a, r_k, ln_x_w, ln_x_b, w_out, g_ffn, w_route_group, b_route_group, w_route_expert, b_route_expert, w_e_gate, w_e_up, w_e_down):
    B, S, D = x.shape
    T = B * S
    depth = g_mix.shape[0]
    xf = x.reshape(T, D)
    vec = lambda t: t.reshape(1, -1)
    for l in range(depth):
        w_dsa, w_rwkv = _prep_in_weights(w_in[l])
        p_dsa = norm_matmul(xf, vec(g_mix[l]), w_dsa, 1024, 256)
        p_rwkv = norm_matmul(xf, vec(g_mix[l]), w_rwkv, 1024, 256)
        q, k, v, qi, ki, w = dsa_prep(p_dsa, vec(q_gain[l]), vec(k_gain[l]), B, S, 256)
        y_dsa = dsa_attention(q, k, v, qi, ki, w, B, S, 256, 256)
        wd, wa, wg = _prep_lora(w_decay_up[l], w_aicl_up[l], w_gate_lora_up[l])
        y_rwkv = rwkv_mixer(p_rwkv, _rwkv_cols(vec(rwkv_shift_mix[l])), vec(w0[l]), vec(a0[l]),
                            vec(k_k[l]), vec(k_a[l]), vec(r_k[l]), vec(ln_x_w[l]), vec(ln_x_b[l]),
                            wd, wa, wg, B, S, CHUNK)
        xf = out_proj(xf, y_dsa, y_rwkv, w_out[l].astype(BF16), 512)
        wr_hi, wr_lo, br = _prep_route(w_route_group[l], b_route_group[l],
                                       w_route_expert[l], b_route_expert[l])
        xf = moe_block(xf, vec(g_ffn[l]), wr_hi, wr_lo, br, w_e_gate[l].astype(BF16),
                       w_e_up[l].astype(BF16), w_e_down[l].astype(BF16), 512)
    return xf.reshape(B, S, D)
```

```python
import functools

import jax
import jax.numpy as jnp
import numpy as np
from jax import lax
from jax.experimental import pallas as pl
from jax.experimental.pallas import tpu as pltpu

F32 = jnp.float32
BF16 = jnp.bfloat16

NORM_EPS = 1e-6
CHUNK = 64
DSA_HEADS = 8
DSA_HEAD_DIM = 128
DSA_WIDTH = DSA_HEADS * DSA_HEAD_DIM
IDX_HEADS = 16
IDX_DIM = 64
TOPK_MAX = 256
ROPE_THETA = 10000.0
RWKV_HEADS = 16
RWKV_HEAD_DIM = 64
RWKV_WIDTH = RWKV_HEADS * RWKV_HEAD_DIM
DECAY_LORA = 64
AICL_LORA = 64
GATE_LORA = 160
GN_EPS = 64e-5
N_GROUPS = 4
EXPERTS_PER_GROUP = 4
N_EXPERTS = 16
D_EXPERT = 512

LANES = 128
SUBLANES = 8
VMEM_LIMIT = 56 * 1024 * 1024

DSA_PAD = 4352
RWKV_PAD = 3584
DSA_SMALL = 4096
RWKV_SMALL = 3072

NEG_INF = float("-inf")
INT_MIN = -(2 ** 31)
LOG2E = 1.4426950408889634


def _cparams(sem):
    return pltpu.CompilerParams(dimension_semantics=sem, vmem_limit_bytes=VMEM_LIMIT)


def _norm_matmul_kernel(x_ref, g_ref, w_ref, o_ref, h_ref):
    @pl.when(pl.program_id(1) == 0)
    def _():
        x = x_ref[...]
        ms = jnp.mean(x * x, axis=-1, keepdims=True)
        h_ref[...] = (x * lax.rsqrt(ms + NORM_EPS) * g_ref[...]).astype(BF16)

    o_ref[...] = jnp.dot(h_ref[...], w_ref[...], preferred_element_type=F32)


def norm_matmul(x, g, w, tm):
    T, D = x.shape
    nt, _, tn = w.shape
    N = nt * tn
    return pl.pallas_call(
        _norm_matmul_kernel,
        grid=(T // tm, nt),
        in_specs=[
            pl.BlockSpec((tm, D), lambda i, j: (i, 0)),
            pl.BlockSpec((1, D), lambda i, j: (0, 0)),
            pl.BlockSpec((None, D, tn), lambda i, j: (j, 0, 0)),
        ],
        out_specs=pl.BlockSpec((tm, tn), lambda i, j: (i, j)),
        out_shape=jax.ShapeDtypeStruct((T, N), F32),
        scratch_shapes=[pltpu.VMEM((tm, D), BF16)],
        compiler_params=_cparams(("parallel", "arbitrary")),
        name="norm_matmul",
    )(x, g, w)


def _rope_tables(S, d):
    half = d // 2
    inv = ROPE_THETA ** (-jnp.arange(half, dtype=F32) * (2.0 / d))
    ang = jnp.arange(S, dtype=F32)[:, None] * inv[None, :]
    cos, sin = jnp.cos(ang), jnp.sin(ang)
    return jnp.concatenate([cos, cos], axis=-1), jnp.concatenate([-sin, sin], axis=-1)


def _dsa_prep_kernel(p_ref, qg_ref, kg_ref, c128_ref, s128_ref, c64_ref, s64_ref,
                     qt_ref, k_ref, vt_ref, qit_ref, ki_ref, wt_ref):
    c128, s128 = c128_ref[...], s128_ref[...]
    c64, s64 = c64_ref[...], s64_ref[...]
    tm = c128.shape[0]
    first_half = (lax.broadcasted_iota(jnp.int32, (tm, LANES), 1) % IDX_DIM) < IDX_DIM // 2

    def rope128(y):
        return y * c128 + pltpu.roll(y, DSA_HEAD_DIM // 2, 1) * s128

    def rope64(y):
        partner = jnp.where(first_half, pltpu.roll(y, LANES - IDX_DIM // 2, 1),
                            pltpu.roll(y, IDX_DIM // 2, 1))
        return y * c64 + partner * s64

    def normed(x, gain):
        ms = jnp.mean(x * x, axis=-1, keepdims=True)
        return x * lax.rsqrt(ms + NORM_EPS) * gain

    scale = DSA_HEAD_DIM ** -0.5 * LOG2E
    for h in range(DSA_HEADS):
        lo = h * DSA_HEAD_DIM
        q = p_ref[:, lo:lo + DSA_HEAD_DIM]
        qt_ref[lo:lo + DSA_HEAD_DIM, :] = (rope128(normed(q, qg_ref[...])) * scale).T.astype(BF16)
        k = p_ref[:, DSA_WIDTH + lo:DSA_WIDTH + lo + DSA_HEAD_DIM]
        k_ref[:, lo:lo + DSA_HEAD_DIM] = rope128(normed(k, kg_ref[...])).astype(BF16)
        v = p_ref[:, 2 * DSA_WIDTH + lo:2 * DSA_WIDTH + lo + DSA_HEAD_DIM]
        vt_ref[lo:lo + DSA_HEAD_DIM, :] = v.T.astype(BF16)
    for j in range(IDX_HEADS * IDX_DIM // LANES):
        lo = 3 * DSA_WIDTH + j * LANES
        qit_ref[j * LANES:(j + 1) * LANES, :] = rope64(p_ref[:, lo:lo + LANES]).T.astype(BF16)
    small = p_ref[:, DSA_SMALL:DSA_SMALL + LANES]
    lane = lax.broadcasted_iota(jnp.int32, (tm, LANES), 1)
    ki = rope64(small)
    ki_ref[:, :LANES] = jnp.where(lane < IDX_DIM, ki, 0.0).astype(BF16)
    ki_ref[:, LANES:] = jnp.where(lane >= IDX_DIM, pltpu.roll(ki, IDX_DIM, 1), 0.0).astype(BF16)
    wt_ref[...] = (small * (IDX_HEADS ** -0.5 * IDX_DIM ** -0.5)).T


def dsa_prep(p_dsa, q_gain, k_gain, B, S, tm):
    T = B * S
    c128, s128 = _rope_tables(S, DSA_HEAD_DIM)
    c64, s64 = _rope_tables(S, IDX_DIM)
    c64 = jnp.concatenate([c64, c64], axis=-1)
    s64 = jnp.concatenate([s64, s64], axis=-1)
    nsb = S // tm
    row = lambda i: (i, 0)
    pos = lambda i: (i % nsb, 0)
    fixed = lambda i: (0, 0)
    tile_t = lambda i: (i // nsb, i % nsb, 0, 0)
    wide_t = pl.BlockSpec((None, None, DSA_WIDTH, tm), tile_t)
    return pl.pallas_call(
        _dsa_prep_kernel,
        grid=(T // tm,),
        in_specs=[
            pl.BlockSpec((tm, DSA_PAD), row),
            pl.BlockSpec((1, DSA_HEAD_DIM), fixed),
            pl.BlockSpec((1, DSA_HEAD_DIM), fixed),
            pl.BlockSpec((tm, LANES), pos),
            pl.BlockSpec((tm, LANES), pos),
            pl.BlockSpec((tm, LANES), pos),
            pl.BlockSpec((tm, LANES), pos),
        ],
        out_specs=[
            wide_t,
            pl.BlockSpec((tm, DSA_WIDTH), row),
            wide_t,
            wide_t,
            pl.BlockSpec((tm, 2 * LANES), row),
            pl.BlockSpec((None, None, LANES, tm), tile_t),
        ],
        out_shape=[
            jax.ShapeDtypeStruct((B, nsb, DSA_WIDTH, tm), BF16),
            jax.ShapeDtypeStruct((T, DSA_WIDTH), BF16),
            jax.ShapeDtypeStruct((B, nsb, DSA_WIDTH, tm), BF16),
            jax.ShapeDtypeStruct((B, nsb, IDX_HEADS * IDX_DIM, tm), BF16),
            jax.ShapeDtypeStruct((T, 2 * LANES), BF16),
            jax.ShapeDtypeStruct((B, nsb, LANES, tm), F32),
        ],
        compiler_params=_cparams(("parallel",)),
        name="dsa_prep",
    )(p_dsa, q_gain, k_gain, c128, s128, c64, s64)


SCORE_ROWS = 128
SEARCH_CAP = 40
PEEL_FROM = 16
INF = float("inf")
SHIFT_BOUND_MAX = 50.0
ONES_ROWS = 16


def _nt_dot(a, b):
    return lax.dot_general(a, b, (((1,), (1,)), ((), ())), preferred_element_type=F32)


def _dsa_kernel(qt_ref, k_ref, vt_ref, qit_ref, ki_ref, wt_ref, y_ref,
                sc_ref, bias_ref, thr_ref, m_ref, l_ref, acc_ref, ksq_ref,
                s0_ref, s1_ref, p0_ref, p1_ref, *, tq, tk, top_k):
    i = pl.program_id(1)
    nkb = ((i + 1) * tq + tk - 1) // tk
    qpos = i * tq + lax.broadcasted_iota(jnp.int32, (1, tq), 1)
    limit = (qpos // CHUNK + 1) * CHUNK
    kf = float(top_k)

    def blk(kb):
        return pl.ds(pl.multiple_of(kb * tk, tk), tk)

    def key_pos(kb, rows, r0=0):
        return kb * tk + r0 + lax.broadcasted_iota(jnp.int32, (rows, tq), 0)

    def score_block(kb, carry):
        mn, mx = carry
        for r0 in range(0, tk, SCORE_ROWS):
            kblk = ki_ref[pl.ds(pl.multiple_of(kb * tk, tk) + r0, SCORE_ROWS), :]
            acc = jnp.zeros((SCORE_ROWS, tq), F32)
            for j in range(IDX_HEADS // 2):
                qblk = qit_ref[j * LANES:(j + 1) * LANES, :]
                s_even = jnp.dot(kblk[:, :LANES], qblk, preferred_element_type=F32)
                s_odd = jnp.dot(kblk[:, LANES:], qblk, preferred_element_type=F32)
                acc = acc + jnp.maximum(s_even, 0.0) * wt_ref[IDX_DIM + 2 * j:IDX_DIM + 2 * j + 1, :]
                acc = acc + jnp.maximum(s_odd, 0.0) * wt_ref[IDX_DIM + 2 * j + 1:IDX_DIM + 2 * j + 2, :]
            adm = key_pos(kb, SCORE_ROWS, r0) < limit
            sc_ref[pl.ds(pl.multiple_of(kb * tk, tk) + r0, SCORE_ROWS), :] = jnp.where(adm, acc, NEG_INF)
            mn = jnp.minimum(mn, jnp.min(jnp.where(adm, acc, INF), axis=0, keepdims=True))
            mx = jnp.maximum(mx, jnp.max(jnp.where(adm, acc, NEG_INF), axis=0, keepdims=True))
        return mn, mx

    row_min, row_max = lax.fori_loop(
        0, nkb, score_block, (jnp.full((1, tq), INF, F32), jnp.full((1, tq), NEG_INF, F32)))

    def fold8(x, op):
        return op(x.reshape(x.shape[0] // SUBLANES, SUBLANES, tq), axis=0)

    def count_where(pred):
        def body(kb, acc):
            return acc + fold8(jnp.where(pred(sc_ref[blk(kb), :]), 1.0, 0.0), jnp.sum)
        part = lax.fori_loop(0, nkb, body, jnp.zeros((SUBLANES, tq), F32))
        return jnp.sum(part, axis=0, keepdims=True)

    def count_ge(x):
        return count_where(lambda s: s >= x)

    def max_below(x):
        def body(kb, acc):
            s = sc_ref[blk(kb), :]
            return jnp.maximum(acc, fold8(jnp.where(s < x, s, NEG_INF), jnp.max))
        part = lax.fori_loop(0, nkb, body, jnp.full((SUBLANES, tq), NEG_INF, F32))
        return jnp.max(part, axis=0, keepdims=True)

    take_all = limit <= top_k

    def search_cond(st):
        it, done = st[0], st[1]
        return jnp.logical_and(it < SEARCH_CAP, jnp.min(done) < 0.5)

    def search_step(st):
        it, done, thr, lo, hi = st
        mid = 0.5 * lo + 0.5 * hi
        peel = jnp.logical_and(it >= PEEL_FROM, it % 4 == 3)
        mid = lax.cond(peel, lambda: max_below(hi), lambda: mid)
        c = count_ge(mid)
        active = done < 0.5
        hit = jnp.logical_or(c == kf, jnp.logical_and(peel, c >= kf))
        fin = jnp.logical_and(active, hit)
        up = jnp.logical_and(active, jnp.logical_and(jnp.logical_not(hit), c > kf))
        dn = jnp.logical_and(active, c < kf)
        return (it + 1, jnp.where(fin, 1.0, done), jnp.where(fin, mid, thr),
                jnp.where(up, mid, lo), jnp.where(dn, mid, hi))

    done0 = jnp.where(take_all, 1.0, 0.0)
    st = lax.while_loop(search_cond, search_step,
                        (jnp.int32(0), done0, jnp.full((1, tq), NEG_INF, F32), row_min,
                         row_max + (jnp.abs(row_max) * 2.0 ** -10 + 1e-30)))
    done, thr = st[1], st[2]
    thr_ref[...] = thr

    @pl.when(jnp.min(done) < 0.5)
    def _():
        def from_key(key):
            return pltpu.bitcast(jnp.where(key < 0, key ^ jnp.int32(0x7FFFFFFF), key), F32)

        def bit_step(b, key):
            cand = key ^ jnp.left_shift(jnp.int32(1), 31 - b)
            return jnp.where(count_ge(from_key(cand)) >= kf, cand, key)

        key = lax.fori_loop(0, 32, bit_step, jnp.full((1, tq), INT_MIN, jnp.int32))
        thr_ref[...] = jnp.where(done < 0.5, from_key(key), thr)

    thr = thr_ref[...]
    tied = jnp.logical_and(count_ge(thr) > kf, jnp.logical_not(take_all))
    any_tied = jnp.max(jnp.where(tied, 1.0, 0.0)) > 0.5

    @pl.when(jnp.logical_not(any_tied))
    def _():
        def body(kb, carry):
            sel = jnp.logical_and(sc_ref[blk(kb), :] >= thr, key_pos(kb, tk) < limit)
            bias_ref[blk(kb), :] = jnp.where(sel, 0.0, NEG_INF)
            return carry
        lax.fori_loop(0, nkb, body, 0)

    @pl.when(any_tied)
    def _():
        need = kf - count_where(lambda s: s > thr)
        r_i = lax.broadcasted_iota(jnp.int32, (tk, tk), 0)
        c_i = lax.broadcasted_iota(jnp.int32, (tk, tk), 1)
        lower = jnp.where(c_i <= r_i, 1.0, 0.0).astype(BF16)

        def body(kb, seen):
            s = sc_ref[blk(kb), :]
            eq = s == thr
            prefix = seen + jnp.dot(lower, jnp.where(eq, 1.0, 0.0).astype(BF16),
                                    preferred_element_type=F32)
            sel = jnp.logical_or(s > thr, jnp.logical_and(eq, prefix <= need))
            sel = jnp.logical_or(sel, take_all)
            sel = jnp.logical_and(sel, key_pos(kb, tk) < limit)
            bias_ref[blk(kb), :] = jnp.where(sel, 0.0, NEG_INF)
            return prefix[tk - 1:tk, :]
        lax.fori_loop(0, nkb, body, jnp.zeros((1, tq), F32))

    heads = [slice(h * DSA_HEAD_DIM, (h + 1) * DSA_HEAD_DIM) for h in range(DSA_HEADS)]

    @pl.when(i == 0)
    def _():
        def body(kb, carry):
            out = []
            for h, hs in enumerate(heads):
                kh = k_ref[blk(kb), hs].astype(F32)
                sq = jnp.max(jnp.sum(kh * kh, axis=1, keepdims=True), axis=0, keepdims=True)
                out.append(jnp.maximum(carry[h], sq))
            return tuple(out)
        ksq = lax.fori_loop(0, k_ref.shape[0] // tk, body,
                            tuple(jnp.zeros((1, 1), F32) for _ in heads))
        for h in range(DSA_HEADS):
            ksq_ref[h] = jnp.broadcast_to(ksq[h], (1, tq))

    worst = jnp.zeros((1, tq), F32)
    for h, hs in enumerate(heads):
        qh = qt_ref[hs, :].astype(F32)
        bound = jnp.sqrt(jnp.sum(qh * qh, axis=0, keepdims=True) * ksq_ref[h])
        m_ref[h] = bound
        worst = jnp.maximum(worst, bound)
    shift_ok = jnp.max(worst) <= SHIFT_BOUND_MAX
    acc_ref[...] = jnp.zeros(acc_ref.shape, F32)

    @pl.when(shift_ok)
    def _():
        ones = jnp.ones((ONES_ROWS, tk), BF16)
        n_blocks = k_ref.shape[0] // tk
        bias_ref[pl.ds(n_blocks * tk, tk), :] = jnp.full((tk, tq), NEG_INF, F32)
        s1_ref[...] = jnp.zeros(s1_ref.shape, F32)
        p0_ref[...] = jnp.zeros(p0_ref.shape, BF16)

        def step(t, s_w, s_r, p_w, p_r):
            ta = jnp.minimum(t, nkb - 1)
            for h, hs in enumerate(heads):
                s_w[h] = jnp.dot(k_ref[blk(ta), hs], qt_ref[hs, :], preferred_element_type=F32)
            tb = t - 1
            valid = jnp.logical_and(tb >= 0, tb < nkb)
            bias = bias_ref[blk(jnp.where(valid, tb, n_blocks)), :]
            for h in range(DSA_HEADS):
                p_w[h] = jnp.exp2(s_r[h] + bias - m_ref[h]).astype(BF16)
            tc = jnp.maximum(t - 2, 0)
            for h, hs in enumerate(heads):
                acc_ref[h] += jnp.dot(jnp.concatenate([vt_ref[tc, hs, :], ones], axis=0), p_r[h],
                                      preferred_element_type=F32)

        def one_step(t, carry):
            @pl.when(t % 2 == 0)
            def _():
                step(t, s0_ref, s1_ref, p1_ref, p0_ref)

            @pl.when(t % 2 == 1)
            def _():
                step(t, s1_ref, s0_ref, p0_ref, p1_ref)
            return carry

        lax.fori_loop(0, nkb + 2, one_step, 0)
        for h, hs in enumerate(heads):
            acc = acc_ref[h]
            y = acc[:DSA_HEAD_DIM] / acc[DSA_HEAD_DIM:DSA_HEAD_DIM + 1]
            y_ref[:, hs] = y.T.astype(y_ref.dtype)

    @pl.when(jnp.logical_not(shift_ok))
    def _():
        m_ref[...] = jnp.full(m_ref.shape, NEG_INF, F32)
        l_ref[...] = jnp.zeros(l_ref.shape, F32)

        def attend(kb, carry):
            bias = bias_ref[blk(kb), :]
            for h, hs in enumerate(heads):
                s = jnp.dot(k_ref[blk(kb), hs], qt_ref[hs, :], preferred_element_type=F32) + bias
                m = m_ref[h]
                m_new = jnp.maximum(m, jnp.max(s, axis=0, keepdims=True))
                m_use = jnp.where(m_new == NEG_INF, 0.0, m_new)
                p = jnp.exp2(s - m_use)
                alpha = jnp.exp2(m - m_use)
                m_ref[h] = m_new
                l_ref[h] = alpha * l_ref[h] + jnp.sum(p, axis=0, keepdims=True)
                acc_ref[h, :DSA_HEAD_DIM, :] = (
                    alpha * acc_ref[h, :DSA_HEAD_DIM, :]
                    + jnp.dot(vt_ref[kb, hs, :], p.astype(BF16), preferred_element_type=F32))
            return carry

        lax.fori_loop(0, nkb, attend, 0)
        for h, hs in enumerate(heads):
            y_ref[:, hs] = (acc_ref[h, :DSA_HEAD_DIM, :] / l_ref[h]).T.astype(y_ref.dtype)


def dsa_attention(qt, k, vt, qit, ki, wt, B, S, tq, tk):
    assert tq == tk and qt.shape[-1] == tq
    T = B * S
    nq = S // tq
    top_k = min(TOPK_MAX, S // 4)
    qtile = lambda b, i: (b, i, 0, 0)
    seq = lambda b, i: (b, 0)
    return pl.pallas_call(
        functools.partial(_dsa_kernel, tq=tq, tk=tk, top_k=top_k),
        grid=(B, nq),
        in_specs=[
            pl.BlockSpec((None, None, DSA_WIDTH, tq), qtile),
            pl.BlockSpec((S, DSA_WIDTH), seq),
            pl.BlockSpec((None, nq, DSA_WIDTH, tk), lambda b, i: (b, 0, 0, 0)),
            pl.BlockSpec((None, None, IDX_HEADS * IDX_DIM, tq), qtile),
            pl.BlockSpec((S, 2 * LANES), seq),
            pl.BlockSpec((None, None, LANES, tq), qtile),
        ],
        out_specs=pl.BlockSpec((tq, DSA_WIDTH), lambda b, i: (b * nq + i, 0)),
        out_shape=jax.ShapeDtypeStruct((T, DSA_WIDTH), BF16),
        scratch_shapes=[pltpu.VMEM((S, tq), F32), pltpu.VMEM((S + tk, tq), F32),
                        pltpu.VMEM((1, tq), F32),
                        pltpu.VMEM((DSA_HEADS, 1, tq), F32), pltpu.VMEM((DSA_HEADS, 1, tq), F32),
                        pltpu.VMEM((DSA_HEADS, DSA_HEAD_DIM + ONES_ROWS, tq), F32),
                        pltpu.VMEM((DSA_HEADS, 1, tq), F32),
                        pltpu.VMEM((DSA_HEADS, tk, tq), F32), pltpu.VMEM((DSA_HEADS, tk, tq), F32),
                        pltpu.VMEM((DSA_HEADS, tk, tq), BF16), pltpu.VMEM((DSA_HEADS, tk, tq), BF16)],
        compiler_params=_cparams(("parallel", "arbitrary")),
        name="dsa_attention",
    )(qt, k, vt, qit, ki, wt)


RW_PAIRS = RWKV_HEADS // 2
RW_GROUPS = RWKV_HEADS // 4


def _split3(x):
    hi = x.astype(BF16)
    r1 = x - hi.astype(F32)
    mid = r1.astype(BF16)
    lo = (r1 - mid.astype(F32)).astype(BF16)
    return hi, mid, lo


def _rwkv_kernel(p_ref, mu_ref, w0_ref, a0_ref, kk_ref, ka_ref, rk_ref, lnw_ref, lnb_ref,
                 wd_ref, wa_ref, wg_ref, y_ref, state_ref, prev_ref, *, C):
    c = pl.program_id(1)

    @pl.when(c == 0)
    def _():
        state_ref[...] = jnp.zeros_like(state_ref)
        prev_ref[...] = jnp.zeros_like(prev_ref)

    W = RWKV_WIDTH
    dot = functools.partial(jnp.dot, preferred_element_type=F32)

    p = p_ref[...]
    first_row = lax.broadcasted_iota(jnp.int32, p.shape, 0) == 0
    prev = jnp.where(first_row, prev_ref[...], pltpu.roll(p, 1, 0))
    prev_ref[...] = p[C - 1:C, :]
    xs = p + (prev - p) * mu_ref[...]

    small = xs[:, RWKV_SMALL:RWKV_SMALL + LANES]
    gate_in = xs[:, RWKV_SMALL + LANES:RWKV_SMALL + 3 * LANES]
    z = w0_ref[...] + dot(jnp.tanh(small).astype(BF16), wd_ref[...])
    softplus = jnp.maximum(-z, 0.0) + jnp.log(1.0 + jnp.exp(-jnp.abs(z)))
    logdec = -jnp.exp(-softplus - 0.5)
    rate = jax.nn.sigmoid(a0_ref[...] + dot(small.astype(BF16), wa_ref[...]))
    gate = dot(jax.nn.sigmoid(gate_in).astype(BF16), wg_ref[...])

    ti = lax.broadcasted_iota(jnp.int32, (C, C), 0)
    tj = lax.broadcasted_iota(jnp.int32, (C, C), 1)
    lower = jnp.where(tj <= ti, 1.0, 0.0).astype(BF16)
    hi, mid, lo = _split3(logdec)
    cum = dot(lower, hi) + dot(lower, mid) + dot(lower, lo)

    gi = lax.broadcasted_iota(jnp.int32, (LANES, LANES), 0) // RWKV_HEAD_DIM
    gj = lax.broadcasted_iota(jnp.int32, (LANES, LANES), 1) // RWKV_HEAD_DIM
    same_head = jnp.where(gi == gj, 1.0, 0.0).astype(BF16)

    def head_sum(parts):
        x = jnp.concatenate(parts, axis=0)
        xh = x.astype(BF16)
        xl = (x - xh.astype(F32)).astype(BF16)
        s = dot(xh, same_head) + dot(xl, same_head)
        return [s[n * C:(n + 1) * C] for n in range(len(parts))]

    lane = lax.broadcasted_iota(jnp.int32, (C, LANES), 1)
    even = lane < RWKV_HEAD_DIM
    zeros = jnp.zeros((C, LANES), F32)

    pairs = []
    for pi in range(RW_PAIRS):
        sl = slice(pi * LANES, (pi + 1) * LANES)
        r = xs[:, sl]
        k = xs[:, W + pi * LANES:W + (pi + 1) * LANES]
        v = xs[:, 2 * W + pi * LANES:2 * W + (pi + 1) * LANES]
        a = rate[:, sl]
        kk = k * kk_ref[:, sl]
        k2 = k * (1.0 + (a - 1.0) * ka_ref[:, sl])
        pairs.append(dict(r=r, v=v, a=a, kk=kk, k2=k2, sl=sl))
    sums = head_sum([d["kk"] * d["kk"] for d in pairs] + [d["r"] * d["k2"] * rk_ref[:, d["sl"]] for d in pairs])
    for pi, d in enumerate(pairs):
        sl = d["sl"]
        kkn = d["kk"] / jnp.maximum(jnp.sqrt(sums[pi]), 1e-12)
        d["bonus"] = sums[RW_PAIRS + pi] * d["v"]
        b = kkn * d["a"]
        cm = cum[:, sl]
        last = cm[C - 1:C, :]
        e_neg = jnp.exp(-cm)
        e_rem = jnp.exp(last - cm)
        d["At"] = -kkn * jnp.exp(cm - logdec[:, sl])
        d["Rt"] = d["r"] * jnp.exp(cm)
        d["Bt"] = b * e_neg
        d["Kt"] = d["k2"] * e_neg
        d["Bh"] = b * e_rem
        d["Kh"] = d["k2"] * e_rem
        d["wlast"] = jnp.exp(last)

    G = 4 * C
    ri = lax.broadcasted_iota(jnp.int32, (G, G), 0)
    ci = lax.broadcasted_iota(jnp.int32, (G, G), 1)
    same_block = (ri // C) == (ci // C)
    strict = jnp.logical_and(same_block, ci < ri)
    incl = jnp.logical_and(same_block, ci <= ri)
    pair_block = jnp.where(gi == gj, 1.0, 0.0)

    def masked4(x0, x1):
        return jnp.concatenate([
            jnp.concatenate([jnp.where(even, x0, 0.0), zeros], axis=1),
            jnp.concatenate([jnp.where(even, 0.0, x0), zeros], axis=1),
            jnp.concatenate([zeros, jnp.where(even, x1, 0.0)], axis=1),
            jnp.concatenate([zeros, jnp.where(even, 0.0, x1)], axis=1)], axis=0).astype(BF16)

    def plain4(x0, x1):
        return jnp.concatenate([
            jnp.concatenate([x0, zeros], axis=1), jnp.concatenate([x0, zeros], axis=1),
            jnp.concatenate([zeros, x1], axis=1), jnp.concatenate([zeros, x1], axis=1)],
            axis=0).astype(BF16)

    groups = []
    for g in range(RW_GROUPS):
        d0, d1 = pairs[2 * g], pairs[2 * g + 1]
        lhs_a = masked4(d0["At"], d1["At"])
        lhs_r = masked4(d0["Rt"], d1["Rt"])
        rhs_b = plain4(d0["Bt"], d1["Bt"])
        rhs_k = plain4(d0["Kt"], d1["Kt"])
        groups.append(dict(
            d=(d0, d1),
            m=jnp.where(strict, _nt_dot(lhs_a, rhs_b), 0.0).astype(BF16),
            m_ak=jnp.where(strict, _nt_dot(lhs_a, rhs_k), 0.0).astype(BF16),
            r_b=jnp.where(incl, _nt_dot(lhs_r, rhs_b), 0.0).astype(BF16),
            r_k=jnp.where(incl, _nt_dot(lhs_r, rhs_k), 0.0).astype(BF16),
            v4=jnp.concatenate([d0["v"], d0["v"], d1["v"], d1["v"]], axis=0).astype(BF16)))
    for g, grp in enumerate(groups):
        a0s, y0s = [], []
        for n, d in enumerate(grp["d"]):
            st = state_ref[2 * g + n].astype(BF16)
            both = _nt_dot(jnp.concatenate([d["At"], d["Rt"]], axis=0).astype(BF16), st)
            a0s.append(both[:C])
            y0s.append(both[C:])
        grp["y0"] = y0s
        grp["x"] = (jnp.concatenate([a0s[0], a0s[0], a0s[1], a0s[1]], axis=0)
                    + dot(grp["m_ak"], grp["v4"]))
    steps = max(1, int(np.ceil(np.log2(C))))
    for s in range(steps):
        for grp in groups:
            grp["x"] = grp["x"] + dot(grp["m"], grp["x"].astype(BF16))
        if s + 1 < steps:
            for grp in groups:
                grp["m"] = dot(grp["m"], grp["m"]).astype(BF16)
    ys = []
    for g, grp in enumerate(groups):
        x = grp["x"]
        yc = dot(grp["r_b"], x.astype(BF16)) + dot(grp["r_k"], grp["v4"])
        for n, d in enumerate(grp["d"]):
            lo_, hi_ = 2 * n * C, (2 * n + 1) * C
            y = grp["y0"][n] + jnp.where(even, yc[lo_:hi_], yc[hi_:hi_ + C])
            u = jnp.where(even, x[lo_:hi_], x[hi_:hi_ + C])
            uv = jnp.concatenate([u, d["v"]], axis=0).astype(BF16)
            bk = jnp.concatenate([d["Bh"], d["Kh"]], axis=0).astype(BF16)
            upd = lax.dot_general(uv, bk, (((0,), (0,)), ((), ())), preferred_element_type=F32)
            state_ref[2 * g + n] = state_ref[2 * g + n] * d["wlast"] + upd * pair_block
            ys.append(y)

    means = head_sum(ys)
    devs = [y - m_ * (1.0 / RWKV_HEAD_DIM) for y, m_ in zip(ys, means)]
    varis = head_sum([dv * dv for dv in devs])
    for pi, d in enumerate(pairs):
        sl = d["sl"]
        yn = devs[pi] * lax.rsqrt(varis[pi] * (1.0 / RWKV_HEAD_DIM) + GN_EPS)
        yn = yn * lnw_ref[:, sl] + lnb_ref[:, sl]
        y_ref[:, sl] = ((yn + d["bonus"]) * gate[:, sl]).astype(y_ref.dtype)


def rwkv_mixer(p_rwkv, mu, w0, a0, k_k, k_a, r_k, ln_w, ln_b, wd, wa, wg, B, S, C):
    T = B * S
    nc = S // C
    vec = pl.BlockSpec((1, RWKV_WIDTH), lambda b, c: (0, 0))
    return pl.pallas_call(
        functools.partial(_rwkv_kernel, C=C),
        grid=(B, nc),
        in_specs=[
            pl.BlockSpec((C, RWKV_PAD), lambda b, c: (b * nc + c, 0)),
            pl.BlockSpec((1, RWKV_PAD), lambda b, c: (0, 0)),
            vec, vec, vec, vec, vec, vec, vec,
            pl.BlockSpec((LANES, RWKV_WIDTH), lambda b, c: (0, 0)),
            pl.BlockSpec((LANES, RWKV_WIDTH), lambda b, c: (0, 0)),
            pl.BlockSpec((2 * LANES, RWKV_WIDTH), lambda b, c: (0, 0)),
        ],
        out_specs=pl.BlockSpec((C, RWKV_WIDTH), lambda b, c: (b * nc + c, 0)),
        out_shape=jax.ShapeDtypeStruct((T, RWKV_WIDTH), BF16),
        scratch_shapes=[pltpu.VMEM((RW_PAIRS, LANES, LANES), F32),
                        pltpu.VMEM((1, RWKV_PAD), F32)],
        compiler_params=_cparams(("parallel", "arbitrary")),
        name="rwkv_mixer",
    )(p_rwkv, mu, w0, a0, k_k, k_a, r_k, ln_w, ln_b, wd, wa, wg)


def _out_proj_kernel(x_ref, ya_ref, yb_ref, wa_ref, wb_ref, o_ref):
    o_ref[...] = (x_ref[...]
                  + jnp.dot(ya_ref[...], wa_ref[...], preferred_element_type=F32)
                  + jnp.dot(yb_ref[...], wb_ref[...], preferred_element_type=F32))


def out_proj(x, ya, yb, w_out, tm):
    T, D = x.shape
    half = ya.shape[1]
    row = lambda i: (i, 0)
    return pl.pallas_call(
        _out_proj_kernel,
        grid=(T // tm,),
        in_specs=[
            pl.BlockSpec((tm, D), row),
            pl.BlockSpec((tm, half), row),
            pl.BlockSpec((tm, half), row),
            pl.BlockSpec((half, D), lambda i: (0, 0)),
            pl.BlockSpec((half, D), lambda i: (1, 0)),
        ],
        out_specs=pl.BlockSpec((tm, D), row),
        out_shape=jax.ShapeDtypeStruct((T, D), F32),
        compiler_params=_cparams(("parallel",)),
        name="out_proj",
    )(x, ya, yb, w_out, w_out)


ROUTE_EXPERT0 = N_GROUPS


def _route(logits):
    lane = lax.broadcasted_iota(jnp.int32, logits.shape, 1)
    big = jnp.int32(LANES)
    is_grp = lane < N_GROUPS
    gl = jnp.where(is_grp, logits, NEG_INF)
    gmax = jnp.max(gl, axis=1, keepdims=True)
    p_grp = 1.0 / jnp.sum(jnp.exp(gl - gmax), axis=1, keepdims=True)
    g_sel = jnp.min(jnp.where(gl == gmax, lane, big), axis=1, keepdims=True)
    lo = ROUTE_EXPERT0 + g_sel * EXPERTS_PER_GROUP
    in_grp = jnp.logical_and(lane >= lo, lane < lo + EXPERTS_PER_GROUP)
    el = jnp.where(in_grp, logits, NEG_INF)
    ee = jnp.exp(el - jnp.max(el, axis=1, keepdims=True))
    pe = jnp.where(in_grp, ee / jnp.sum(ee, axis=1, keepdims=True), -1.0)
    p1 = jnp.max(pe, axis=1, keepdims=True)
    i1 = jnp.min(jnp.where(pe == p1, lane, big), axis=1, keepdims=True)
    pe2 = jnp.where(lane == i1, -1.0, pe)
    p2 = jnp.max(pe2, axis=1, keepdims=True)
    i2 = jnp.min(jnp.where(pe2 == p2, lane, big), axis=1, keepdims=True)
    tot = p1 + p2
    comb = jnp.where(lane == i1, p_grp * (p1 / tot), jnp.where(lane == i2, p_grp * (p2 / tot), 0.0))
    return comb, g_sel


MOE_TM = 1024
MOE_SUB = 256
MOE_PAD = (MOE_TM + N_GROUPS * (MOE_SUB - 1)) // MOE_SUB * MOE_SUB


def _moe_kernel(x_ref, g_ref, wr_hi_ref, wr_lo_ref, br_ref, wg_ref, wu_ref, wd_ref, o_ref,
                hs_ref, ys_ref, cs_ref, pos_ref, meta_ref):
    e = pl.program_id(1)
    tm = x_ref.shape[0]
    dot = functools.partial(jnp.dot, preferred_element_type=F32)

    @pl.when(e == 0)
    def _():
        x = x_ref[...]
        ms = jnp.mean(x * x, axis=-1, keepdims=True)
        h = x * lax.rsqrt(ms + NORM_EPS) * g_ref[...]
        hi = h.astype(BF16)
        lo = (h - hi.astype(F32)).astype(BF16)
        logits = dot(hi, wr_hi_ref[...]) + dot(lo, wr_hi_ref[...]) + dot(hi, wr_lo_ref[...])
        comb, g_sel = _route(logits + br_ref[...])

        lane = lax.broadcasted_iota(jnp.int32, (tm, LANES), 1)
        onehot = jnp.where(lane == g_sel, 1.0, 0.0)
        onehot_b = onehot.astype(BF16)
        ranks = []
        for c0 in range(0, tm, MOE_SUB):
            t_i = c0 + lax.broadcasted_iota(jnp.int32, (MOE_SUB, tm), 0)
            t_j = lax.broadcasted_iota(jnp.int32, (MOE_SUB, tm), 1)
            ranks.append(dot(jnp.where(t_j < t_i, 1.0, 0.0).astype(BF16), onehot_b))
        rank = jnp.concatenate(ranks, axis=0)
        counts = jnp.sum(onehot, axis=0, keepdims=True)
        padded = jnp.floor((counts + (MOE_SUB - 1.0)) * (1.0 / MOE_SUB)) * MOE_SUB
        lane1 = lax.broadcasted_iota(jnp.int32, (1, LANES), 1)
        base = jnp.zeros((1, LANES), F32)
        start = jnp.zeros((1, 1), F32)
        for g in range(N_GROUPS):
            pad_g = jnp.sum(jnp.where(lane1 == g, padded, 0.0), axis=1, keepdims=True)
            base = jnp.where(lane1 == g, start, base)
            meta_ref[g] = (start[0, 0] * (1.0 / MOE_SUB)).astype(jnp.int32)
            meta_ref[N_GROUPS + g] = (pad_g[0, 0] * (1.0 / MOE_SUB)).astype(jnp.int32)
            start = start + pad_g
        pos = jnp.sum(onehot * (rank + base), axis=1, keepdims=True)
        pos_b = jnp.broadcast_to(pos, (tm, LANES))
        pos_ref[...] = pos_b
        pos_row = pos_b.T[0:1, :]
        c_hi, c_mid, c_lo = _split3(comb)
        for c0 in range(0, MOE_PAD, MOE_SUB):
            r_i = (c0 + lax.broadcasted_iota(jnp.int32, (MOE_SUB, tm), 0)).astype(F32)
            perm = jnp.where(r_i == pos_row, 1.0, 0.0).astype(BF16)
            hs_ref[c0:c0 + MOE_SUB, :] = dot(perm, hi).astype(BF16)
            cs_ref[c0:c0 + MOE_SUB, :] = dot(perm, c_hi) + dot(perm, c_mid) + dot(perm, c_lo)
        ys_ref[...] = jnp.zeros(ys_ref.shape, BF16)
        o_ref[...] = x

    grp = e // EXPERTS_PER_GROUP
    first = meta_ref[grp]
    lane_s = lax.broadcasted_iota(jnp.int32, (MOE_SUB, LANES), 1)

    def sub_tile(si, carry):
        rows = pl.ds(pl.multiple_of((first + si) * MOE_SUB, MOE_SUB), MOE_SUB)
        hb = hs_ref[rows, :]
        hg = dot(hb, wg_ref[...])
        hu = dot(hb, wu_ref[...])
        he = (hg * jax.nn.sigmoid(hg) * hu).astype(BF16)
        ce = jnp.sum(jnp.where(lane_s == e + ROUTE_EXPERT0, cs_ref[rows, :], 0.0), axis=1, keepdims=True)
        ys_ref[rows, :] = (ys_ref[rows, :].astype(F32) + ce * dot(he, wd_ref[...])).astype(BF16)
        return carry

    lax.fori_loop(0, meta_ref[N_GROUPS + grp], sub_tile, 0)

    @pl.when(e == pl.num_programs(1) - 1)
    def _():
        c_i = lax.broadcasted_iota(jnp.int32, (MOE_SUB, MOE_PAD), 1).astype(F32)
        for c0 in range(0, tm, MOE_SUB):
            unperm = jnp.where(pos_ref[c0:c0 + MOE_SUB, 0:1] == c_i, 1.0, 0.0).astype(BF16)
            o_ref[c0:c0 + MOE_SUB, :] += dot(unperm, ys_ref[...])


def moe_block(x, g, wr_hi, wr_lo, br, w_gate, w_up, w_down):
    T, D = x.shape
    E, _, Fd = w_gate.shape
    tm = MOE_TM
    row = lambda i, e: (i, 0)
    fixed = lambda i, e: (0, 0)
    return pl.pallas_call(
        _moe_kernel,
        grid=(T // tm, E),
        in_specs=[
            pl.BlockSpec((tm, D), row, pipeline_mode=pl.Buffered(1)),
            pl.BlockSpec((1, D), fixed),
            pl.BlockSpec((D, LANES), fixed),
            pl.BlockSpec((D, LANES), fixed),
            pl.BlockSpec((1, LANES), fixed),
            pl.BlockSpec((None, D, Fd), lambda i, e: (e, 0, 0)),
            pl.BlockSpec((None, D, Fd), lambda i, e: (e, 0, 0)),
            pl.BlockSpec((None, Fd, D), lambda i, e: (e, 0, 0)),
        ],
        out_specs=pl.BlockSpec((tm, D), row, pipeline_mode=pl.Buffered(1)),
        out_shape=jax.ShapeDtypeStruct((T, D), F32),
        scratch_shapes=[pltpu.VMEM((MOE_PAD, D), BF16), pltpu.VMEM((MOE_PAD, D), BF16),
                        pltpu.VMEM((MOE_PAD, LANES), F32), pltpu.VMEM((tm, LANES), F32),
                        pltpu.SMEM((2 * N_GROUPS,), jnp.int32)],
        compiler_params=_cparams(("parallel", "arbitrary")),
        name="moe_block",
    )(x, g, wr_hi, wr_lo, br, w_gate, w_up, w_down)


def _prep_route(w_route_group, b_route_group, w_route_expert, b_route_expert):
    D = w_route_group.shape[0]
    pad = LANES - N_GROUPS - N_EXPERTS
    wr = jnp.concatenate([w_route_group, w_route_expert, jnp.zeros((D, pad), F32)], axis=1)
    br = jnp.concatenate([b_route_group, b_route_expert, jnp.zeros((pad,), F32)])[None]
    hi = wr.astype(BF16)
    lo = (wr - hi.astype(F32)).astype(BF16)
    return hi, lo, br


def _prep_lora(w_decay_up, w_aicl_up, w_gate_lora_up):
    zw = jnp.zeros((LANES - DECAY_LORA, RWKV_WIDTH), w_decay_up.dtype)
    wd = jnp.concatenate([w_decay_up, zw], axis=0)
    wa = jnp.concatenate([jnp.zeros((DECAY_LORA, RWKV_WIDTH), w_aicl_up.dtype), w_aicl_up], axis=0)
    wg = jnp.concatenate([w_gate_lora_up,
                          jnp.zeros((2 * LANES - GATE_LORA, RWKV_WIDTH), w_gate_lora_up.dtype)], axis=0)
    return wd.astype(BF16), wa.astype(BF16), wg.astype(BF16)


DSA_COLS = 4176
IN_PROJ_TILES = 2
IN_PROJ_ROWS = 512
RWKV_SPLITS = (RWKV_WIDTH, DECAY_LORA, RWKV_WIDTH, RWKV_WIDTH, AICL_LORA, GATE_LORA)


def _rwkv_cols(t):
    cuts = np.cumsum(RWKV_SPLITS)[:-1].tolist()
    r, dw, k, v, da, dg = jnp.split(t, cuts, axis=-1)
    pad = jnp.zeros(t.shape[:-1] + (RWKV_PAD - sum(RWKV_SPLITS),), t.dtype)
    return jnp.concatenate([r, k, v, dw, da, dg, pad], axis=-1)


def _prep_in_weights(w_in):
    pad = jnp.zeros((w_in.shape[0], DSA_PAD - DSA_COLS), w_in.dtype)
    w_dsa = jnp.concatenate([w_in[:, :DSA_COLS], pad], axis=1)
    w_rwkv = _rwkv_cols(w_in[:, DSA_COLS:])

    def tiles(w):
        d, n = w.shape
        return w.astype(BF16).reshape(d, IN_PROJ_TILES, n // IN_PROJ_TILES).transpose(1, 0, 2)

    return tiles(w_dsa), tiles(w_rwkv)


def kernel(x, g_mix, w_in, rwkv_shift_mix, q_gain, k_gain, w0, w_decay_up, a0, w_aicl_up, w_gate_lora_up, k_k, k_a, r_k, ln_x_w, ln_x_b, w_out, g_ffn, w_route_group, b_route_group, w_route_expert, b_route_expert, w_e_gate, w_e_up, w_e_down):
    B, S, D = x.shape
    T = B * S
    depth = g_mix.shape[0]
    xf = x.reshape(T, D)
    vec = lambda t: t.reshape(1, -1)
    for l in range(depth):
        w_dsa, w_rwkv = _prep_in_weights(w_in[l])
        p_dsa = norm_matmul(xf, vec(g_mix[l]), w_dsa, IN_PROJ_ROWS)
        p_rwkv = norm_matmul(xf, vec(g_mix[l]), w_rwkv, IN_PROJ_ROWS)
        q, k, v, qi, ki, w = dsa_prep(p_dsa, vec(q_gain[l]), vec(k_gain[l]), B, S, 256)
        y_dsa = dsa_attention(q, k, v, qi, ki, w, B, S, 256, 256)
        wd, wa, wg = _prep_lora(w_decay_up[l], w_aicl_up[l], w_gate_lora_up[l])
        y_rwkv = rwkv_mixer(p_rwkv, _rwkv_cols(vec(rwkv_shift_mix[l])), vec(w0[l]), vec(a0[l]),
                            vec(k_k[l]), vec(k_a[l]), vec(r_k[l]), vec(ln_x_w[l]), vec(ln_x_b[l]),
                            wd, wa, wg, B, S, CHUNK)
        xf = out_proj(xf, y_dsa, y_rwkv, w_out[l].astype(BF16), 512)
        wr_hi, wr_lo, br = _prep_route(w_route_group[l], b_route_group[l],
                                       w_route_expert[l], b_route_expert[l])
        xf = moe_block(xf, vec(g_ffn[l]), wr_hi, wr_lo, br, w_e_gate[l].astype(BF16),
                       w_e_up[l].astype(BF16), w_e_down[l].astype(BF16))
    return xf.reshape(B, S, D)
```

```python
import functools

import jax
import jax.numpy as jnp
import numpy as np
from jax import lax
from jax.experimental import pallas as pl
from jax.experimental.pallas import tpu as pltpu

F32 = jnp.float32
BF16 = jnp.bfloat16

NORM_EPS = 1e-6
CHUNK = 64
DSA_HEADS = 8
DSA_HEAD_DIM = 128
DSA_WIDTH = DSA_HEADS * DSA_HEAD_DIM
IDX_HEADS = 16
IDX_DIM = 64
TOPK_MAX = 256
ROPE_THETA = 10000.0
RWKV_HEADS = 16
RWKV_HEAD_DIM = 64
RWKV_WIDTH = RWKV_HEADS * RWKV_HEAD_DIM
DECAY_LORA = 64
AICL_LORA = 64
GATE_LORA = 160
GN_EPS = 64e-5
N_GROUPS = 4
EXPERTS_PER_GROUP = 4
N_EXPERTS = 16
D_EXPERT = 512

LANES = 128
SUBLANES = 8
VMEM_LIMIT = 56 * 1024 * 1024

DSA_PAD = 4352
RWKV_PAD = 3584
DSA_SMALL = 4096
RWKV_SMALL = 3072

NEG_INF = float("-inf")
INT_MIN = -(2 ** 31)
LOG2E = 1.4426950408889634


def _cparams(sem):
    return pltpu.CompilerParams(dimension_semantics=sem, vmem_limit_bytes=VMEM_LIMIT)


def _norm_matmul_kernel(x_ref, g_ref, w_ref, o_ref, h_ref):
    @pl.when(pl.program_id(1) == 0)
    def _():
        x = x_ref[...]
        ms = jnp.mean(x * x, axis=-1, keepdims=True)
        h_ref[...] = (x * lax.rsqrt(ms + NORM_EPS) * g_ref[...]).astype(BF16)

    o_ref[...] = jnp.dot(h_ref[...], w_ref[...], preferred_element_type=F32)


def norm_matmul(x, g, w, tm):
    T, D = x.shape
    nt, _, tn = w.shape
    N = nt * tn
    return pl.pallas_call(
        _norm_matmul_kernel,
        grid=(T // tm, nt),
        in_specs=[
            pl.BlockSpec((tm, D), lambda i, j: (i, 0)),
            pl.BlockSpec((1, D), lambda i, j: (0, 0)),
            pl.BlockSpec((None, D, tn), lambda i, j: (j, 0, 0)),
        ],
        out_specs=pl.BlockSpec((tm, tn), lambda i, j: (i, j)),
        out_shape=jax.ShapeDtypeStruct((T, N), F32),
        scratch_shapes=[pltpu.VMEM((tm, D), BF16)],
        compiler_params=_cparams(("parallel", "arbitrary")),
        name="norm_matmul",
    )(x, g, w)


def _rope_tables(S, d):
    half = d // 2
    inv = ROPE_THETA ** (-jnp.arange(half, dtype=F32) * (2.0 / d))
    ang = jnp.arange(S, dtype=F32)[:, None] * inv[None, :]
    cos, sin = jnp.cos(ang), jnp.sin(ang)
    return jnp.concatenate([cos, cos], axis=-1), jnp.concatenate([-sin, sin], axis=-1)


def _dsa_prep_kernel(p_ref, qg_ref, kg_ref, c128_ref, s128_ref, c64_ref, s64_ref,
                     qt_ref, k_ref, vt_ref, qit_ref, ki_ref, wt_ref):
    c128, s128 = c128_ref[...], s128_ref[...]
    c64, s64 = c64_ref[...], s64_ref[...]
    tm = c128.shape[0]
    first_half = (lax.broadcasted_iota(jnp.int32, (tm, LANES), 1) % IDX_DIM) < IDX_DIM // 2

    def rope128(y):
        return y * c128 + pltpu.roll(y, DSA_HEAD_DIM // 2, 1) * s128

    def rope64(y):
        partner = jnp.where(first_half, pltpu.roll(y, LANES - IDX_DIM // 2, 1),
                            pltpu.roll(y, IDX_DIM // 2, 1))
        return y * c64 + partner * s64

    ones = jnp.ones((LANES, LANES), BF16)
    e_i = lax.broadcasted_iota(jnp.int32, (LANES, LANES), 0)
    e_j = lax.broadcasted_iota(jnp.int32, (LANES, LANES), 1)
    eye = jnp.where(e_i == e_j, 1.0, 0.0).astype(BF16)

    def normed(x, gain):
        sq = x * x
        hi = sq.astype(BF16)
        lo = (sq - hi.astype(F32)).astype(BF16)
        ms = (jnp.dot(hi, ones, preferred_element_type=F32)
              + jnp.dot(lo, ones, preferred_element_type=F32)) * (1.0 / DSA_HEAD_DIM)
        return x * lax.rsqrt(ms + NORM_EPS) * gain

    def transposed(y):
        return _nt_dot(eye, y.astype(BF16)).astype(BF16)

    scale = DSA_HEAD_DIM ** -0.5 * LOG2E
    for h in range(DSA_HEADS):
        lo = h * DSA_HEAD_DIM
        q = p_ref[:, lo:lo + DSA_HEAD_DIM]
        qt_ref[lo:lo + DSA_HEAD_DIM, :] = transposed(rope128(normed(q, qg_ref[...])) * scale)
        k = p_ref[:, DSA_WIDTH + lo:DSA_WIDTH + lo + DSA_HEAD_DIM]
        k_ref[:, lo:lo + DSA_HEAD_DIM] = rope128(normed(k, kg_ref[...])).astype(BF16)
        v = p_ref[:, 2 * DSA_WIDTH + lo:2 * DSA_WIDTH + lo + DSA_HEAD_DIM]
        vt_ref[lo:lo + DSA_HEAD_DIM, :] = transposed(v)
    for j in range(IDX_HEADS * IDX_DIM // LANES):
        lo = 3 * DSA_WIDTH + j * LANES
        qit_ref[j * LANES:(j + 1) * LANES, :] = transposed(rope64(p_ref[:, lo:lo + LANES]))
    small = p_ref[:, DSA_SMALL:DSA_SMALL + LANES]
    lane = lax.broadcasted_iota(jnp.int32, (tm, LANES), 1)
    ki = rope64(small)
    ki_ref[:, :LANES] = jnp.where(lane < IDX_DIM, ki, 0.0).astype(BF16)
    ki_ref[:, LANES:] = jnp.where(lane >= IDX_DIM, pltpu.roll(ki, IDX_DIM, 1), 0.0).astype(BF16)
    wt_ref[...] = (small * (IDX_HEADS ** -0.5 * IDX_DIM ** -0.5)).T


def dsa_prep(p_dsa, q_gain, k_gain, B, S, tm):
    T = B * S
    c128, s128 = _rope_tables(S, DSA_HEAD_DIM)
    c64, s64 = _rope_tables(S, IDX_DIM)
    c64 = jnp.concatenate([c64, c64], axis=-1)
    s64 = jnp.concatenate([s64, s64], axis=-1)
    nsb = S // tm
    row = lambda i: (i, 0)
    pos = lambda i: (i % nsb, 0)
    fixed = lambda i: (0, 0)
    tile_t = lambda i: (i // nsb, i % nsb, 0, 0)
    wide_t = pl.BlockSpec((None, None, DSA_WIDTH, tm), tile_t)
    return pl.pallas_call(
        _dsa_prep_kernel,
        grid=(T // tm,),
        in_specs=[
            pl.BlockSpec((tm, DSA_PAD), row),
            pl.BlockSpec((1, DSA_HEAD_DIM), fixed),
            pl.BlockSpec((1, DSA_HEAD_DIM), fixed),
            pl.BlockSpec((tm, LANES), pos),
            pl.BlockSpec((tm, LANES), pos),
            pl.BlockSpec((tm, LANES), pos),
            pl.BlockSpec((tm, LANES), pos),
        ],
        out_specs=[
            wide_t,
            pl.BlockSpec((tm, DSA_WIDTH), row),
            wide_t,
            wide_t,
            pl.BlockSpec((tm, 2 * LANES), row),
            pl.BlockSpec((None, None, LANES, tm), tile_t),
        ],
        out_shape=[
            jax.ShapeDtypeStruct((B, nsb, DSA_WIDTH, tm), BF16),
            jax.ShapeDtypeStruct((T, DSA_WIDTH), BF16),
            jax.ShapeDtypeStruct((B, nsb, DSA_WIDTH, tm), BF16),
            jax.ShapeDtypeStruct((B, nsb, IDX_HEADS * IDX_DIM, tm), BF16),
            jax.ShapeDtypeStruct((T, 2 * LANES), BF16),
            jax.ShapeDtypeStruct((B, nsb, LANES, tm), F32),
        ],
        compiler_params=_cparams(("parallel",)),
        name="dsa_prep",
    )(p_dsa, q_gain, k_gain, c128, s128, c64, s64)


SCORE_ROWS = 128
SEARCH_CAP = 40
PEEL_FROM = 16
INF = float("inf")
SHIFT_BOUND_MAX = 50.0
ONES_ROWS = 16


def _nt_dot(a, b):
    return lax.dot_general(a, b, (((1,), (1,)), ((), ())), preferred_element_type=F32)


def _dsa_kernel(qt_ref, k_ref, vt_ref, qit_ref, ki_ref, wt_ref, y_ref,
                sc_ref, bias_ref, thr_ref, m_ref, l_ref, acc_ref, ksq_ref,
                s0_ref, s1_ref, p0_ref, p1_ref, *, tq, tk, top_k):
    i = pl.program_id(1)
    nkb = ((i + 1) * tq + tk - 1) // tk
    qpos = i * tq + lax.broadcasted_iota(jnp.int32, (1, tq), 1)
    limit = (qpos // CHUNK + 1) * CHUNK
    kf = float(top_k)

    def blk(kb):
        return pl.ds(pl.multiple_of(kb * tk, tk), tk)

    def key_pos(kb, rows, r0=0):
        return kb * tk + r0 + lax.broadcasted_iota(jnp.int32, (rows, tq), 0)

    def score_block(kb, carry):
        mn, mx = carry
        for r0 in range(0, tk, SCORE_ROWS):
            kblk = ki_ref[pl.ds(pl.multiple_of(kb * tk, tk) + r0, SCORE_ROWS), :]
            acc = jnp.zeros((SCORE_ROWS, tq), F32)
            for j in range(IDX_HEADS // 2):
                qblk = qit_ref[j * LANES:(j + 1) * LANES, :]
                s_even = jnp.dot(kblk[:, :LANES], qblk, preferred_element_type=F32)
                s_odd = jnp.dot(kblk[:, LANES:], qblk, preferred_element_type=F32)
                acc = acc + jnp.maximum(s_even, 0.0) * wt_ref[IDX_DIM + 2 * j:IDX_DIM + 2 * j + 1, :]
                acc = acc + jnp.maximum(s_odd, 0.0) * wt_ref[IDX_DIM + 2 * j + 1:IDX_DIM + 2 * j + 2, :]
            adm = key_pos(kb, SCORE_ROWS, r0) < limit
            sc_ref[pl.ds(pl.multiple_of(kb * tk, tk) + r0, SCORE_ROWS), :] = jnp.where(adm, acc, NEG_INF)
            mn = jnp.minimum(mn, jnp.min(jnp.where(adm, acc, INF), axis=0, keepdims=True))
            mx = jnp.maximum(mx, jnp.max(jnp.where(adm, acc, NEG_INF), axis=0, keepdims=True))
        return mn, mx

    row_min, row_max = lax.fori_loop(
        0, nkb, score_block, (jnp.full((1, tq), INF, F32), jnp.full((1, tq), NEG_INF, F32)))

    def fold8(x, op):
        return op(x.reshape(x.shape[0] // SUBLANES, SUBLANES, tq), axis=0)

    def count_where(pred):
        def body(kb, acc):
            return acc + fold8(jnp.where(pred(sc_ref[blk(kb), :]), 1.0, 0.0), jnp.sum)
        part = lax.fori_loop(0, nkb, body, jnp.zeros((SUBLANES, tq), F32))
        return jnp.sum(part, axis=0, keepdims=True)

    def count_ge(x):
        return count_where(lambda s: s >= x)

    def max_below(x):
        def body(kb, acc):
            s = sc_ref[blk(kb), :]
            return jnp.maximum(acc, fold8(jnp.where(s < x, s, NEG_INF), jnp.max))
        part = lax.fori_loop(0, nkb, body, jnp.full((SUBLANES, tq), NEG_INF, F32))
        return jnp.max(part, axis=0, keepdims=True)

    take_all = limit <= top_k

    def search_cond(st):
        it, done = st[0], st[1]
        return jnp.logical_and(it < SEARCH_CAP, jnp.min(done) < 0.5)

    def search_step(st):
        it, done, thr, lo, hi = st
        mid = 0.5 * lo + 0.5 * hi
        peel = jnp.logical_and(it >= PEEL_FROM, it % 4 == 3)
        mid = lax.cond(peel, lambda: max_below(hi), lambda: mid)
        c = count_ge(mid)
        active = done < 0.5
        hit = jnp.logical_or(c == kf, jnp.logical_and(peel, c >= kf))
        fin = jnp.logical_and(active, hit)
        up = jnp.logical_and(active, jnp.logical_and(jnp.logical_not(hit), c > kf))
        dn = jnp.logical_and(active, c < kf)
        return (it + 1, jnp.where(fin, 1.0, done), jnp.where(fin, mid, thr),
                jnp.where(up, mid, lo), jnp.where(dn, mid, hi))

    done0 = jnp.where(take_all, 1.0, 0.0)
    st = lax.while_loop(search_cond, search_step,
                        (jnp.int32(0), done0, jnp.full((1, tq), NEG_INF, F32), row_min,
                         row_max + (jnp.abs(row_max) * 2.0 ** -10 + 1e-30)))
    done, thr = st[1], st[2]
    thr_ref[...] = thr

    @pl.when(jnp.min(done) < 0.5)
    def _():
        def from_key(key):
            return pltpu.bitcast(jnp.where(key < 0, key ^ jnp.int32(0x7FFFFFFF), key), F32)

        def bit_step(b, key):
            cand = key ^ jnp.left_shift(jnp.int32(1), 31 - b)
            return jnp.where(count_ge(from_key(cand)) >= kf, cand, key)

        key = lax.fori_loop(0, 32, bit_step, jnp.full((1, tq), INT_MIN, jnp.int32))
        thr_ref[...] = jnp.where(done < 0.5, from_key(key), thr)

    thr = thr_ref[...]
    tied = jnp.logical_and(count_ge(thr) > kf, jnp.logical_not(take_all))
    any_tied = jnp.max(jnp.where(tied, 1.0, 0.0)) > 0.5

    @pl.when(jnp.logical_not(any_tied))
    def _():
        def body(kb, carry):
            sel = jnp.logical_and(sc_ref[blk(kb), :] >= thr, key_pos(kb, tk) < limit)
            bias_ref[blk(kb), :] = jnp.where(sel, 0.0, NEG_INF)
            return carry
        lax.fori_loop(0, nkb, body, 0)

    @pl.when(any_tied)
    def _():
        need = kf - count_where(lambda s: s > thr)
        r_i = lax.broadcasted_iota(jnp.int32, (tk, tk), 0)
        c_i = lax.broadcasted_iota(jnp.int32, (tk, tk), 1)
        lower = jnp.where(c_i <= r_i, 1.0, 0.0).astype(BF16)

        def body(kb, seen):
            s = sc_ref[blk(kb), :]
            eq = s == thr
            prefix = seen + jnp.dot(lower, jnp.where(eq, 1.0, 0.0).astype(BF16),
                                    preferred_element_type=F32)
            sel = jnp.logical_or(s > thr, jnp.logical_and(eq, prefix <= need))
            sel = jnp.logical_or(sel, take_all)
            sel = jnp.logical_and(sel, key_pos(kb, tk) < limit)
            bias_ref[blk(kb), :] = jnp.where(sel, 0.0, NEG_INF)
            return prefix[tk - 1:tk, :]
        lax.fori_loop(0, nkb, body, jnp.zeros((1, tq), F32))

    heads = [slice(h * DSA_HEAD_DIM, (h + 1) * DSA_HEAD_DIM) for h in range(DSA_HEADS)]

    @pl.when(i == 0)
    def _():
        def body(kb, carry):
            out = []
            for h, hs in enumerate(heads):
                kh = k_ref[blk(kb), hs].astype(F32)
                sq = jnp.max(jnp.sum(kh * kh, axis=1, keepdims=True), axis=0, keepdims=True)
                out.append(jnp.maximum(carry[h], sq))
            return tuple(out)
        ksq = lax.fori_loop(0, k_ref.shape[0] // tk, body,
                            tuple(jnp.zeros((1, 1), F32) for _ in heads))
        for h in range(DSA_HEADS):
            ksq_ref[h] = jnp.broadcast_to(ksq[h], (1, tq))

    worst = jnp.zeros((1, tq), F32)
    for h, hs in enumerate(heads):
        qh = qt_ref[hs, :].astype(F32)
        bound = jnp.sqrt(jnp.sum(qh * qh, axis=0, keepdims=True) * ksq_ref[h])
        m_ref[h] = bound
        worst = jnp.maximum(worst, bound)
    shift_ok = jnp.max(worst) <= SHIFT_BOUND_MAX
    acc_ref[...] = jnp.zeros(acc_ref.shape, F32)

    @pl.when(shift_ok)
    def _():
        ones = jnp.ones((ONES_ROWS, tk), BF16)
        n_blocks = k_ref.shape[0] // tk
        bias_ref[pl.ds(n_blocks * tk, tk), :] = jnp.full((tk, tq), NEG_INF, F32)
        s1_ref[...] = jnp.zeros(s1_ref.shape, F32)
        p0_ref[...] = jnp.zeros(p0_ref.shape, BF16)

        def step(t, s_w, s_r, p_w, p_r):
            ta = jnp.minimum(t, nkb - 1)
            for h, hs in enumerate(heads):
                s_w[h] = jnp.dot(k_ref[blk(ta), hs], qt_ref[hs, :], preferred_element_type=F32)
            tb = t - 1
            valid = jnp.logical_and(tb >= 0, tb < nkb)
            bias = bias_ref[blk(jnp.where(valid, tb, n_blocks)), :]
            for h in range(DSA_HEADS):
                p_w[h] = jnp.exp2(s_r[h] + bias - m_ref[h]).astype(BF16)
            tc = jnp.maximum(t - 2, 0)
            for h, hs in enumerate(heads):
                acc_ref[h] += jnp.dot(jnp.concatenate([vt_ref[tc, hs, :], ones], axis=0), p_r[h],
                                      preferred_element_type=F32)

        def one_step(t, carry):
            @pl.when(t % 2 == 0)
            def _():
                step(t, s0_ref, s1_ref, p1_ref, p0_ref)

            @pl.when(t % 2 == 1)
            def _():
                step(t, s1_ref, s0_ref, p0_ref, p1_ref)
            return carry

        lax.fori_loop(0, nkb + 2, one_step, 0)
        for h, hs in enumerate(heads):
            acc = acc_ref[h]
            y = acc[:DSA_HEAD_DIM] / acc[DSA_HEAD_DIM:DSA_HEAD_DIM + 1]
            y_ref[:, hs] = y.T.astype(y_ref.dtype)

    @pl.when(jnp.logical_not(shift_ok))
    def _():
        m_ref[...] = jnp.full(m_ref.shape, NEG_INF, F32)
        l_ref[...] = jnp.zeros(l_ref.shape, F32)

        def attend(kb, carry):
            bias = bias_ref[blk(kb), :]
            for h, hs in enumerate(heads):
                s = jnp.dot(k_ref[blk(kb), hs], qt_ref[hs, :], preferred_element_type=F32) + bias
                m = m_ref[h]
                m_new = jnp.maximum(m, jnp.max(s, axis=0, keepdims=True))
                m_use = jnp.where(m_new == NEG_INF, 0.0, m_new)
                p = jnp.exp2(s - m_use)
                alpha = jnp.exp2(m - m_use)
                m_ref[h] = m_new
                l_ref[h] = alpha * l_ref[h] + jnp.sum(p, axis=0, keepdims=True)
                acc_ref[h, :DSA_HEAD_DIM, :] = (
                    alpha * acc_ref[h, :DSA_HEAD_DIM, :]
                    + jnp.dot(vt_ref[kb, hs, :], p.astype(BF16), preferred_element_type=F32))
            return carry

        lax.fori_loop(0, nkb, attend, 0)
        for h, hs in enumerate(heads):
            y_ref[:, hs] = (acc_ref[h, :DSA_HEAD_DIM, :] / l_ref[h]).T.astype(y_ref.dtype)


def dsa_attention(qt, k, vt, qit, ki, wt, B, S, tq, tk):
    assert tq == tk and qt.shape[-1] == tq
    T = B * S
    nq = S // tq
    top_k = min(TOPK_MAX, S // 4)
    qtile = lambda b, i: (b, i, 0, 0)
    seq = lambda b, i: (b, 0)
    return pl.pallas_call(
        functools.partial(_dsa_kernel, tq=tq, tk=tk, top_k=top_k),
        grid=(B, nq),
        in_specs=[
            pl.BlockSpec((None, None, DSA_WIDTH, tq), qtile),
            pl.BlockSpec((S, DSA_WIDTH), seq),
            pl.BlockSpec((None, nq, DSA_WIDTH, tk), lambda b, i: (b, 0, 0, 0)),
            pl.BlockSpec((None, None, IDX_HEADS * IDX_DIM, tq), qtile),
            pl.BlockSpec((S, 2 * LANES), seq),
            pl.BlockSpec((None, None, LANES, tq), qtile),
        ],
        out_specs=pl.BlockSpec((tq, DSA_WIDTH), lambda b, i: (b * nq + i, 0)),
        out_shape=jax.ShapeDtypeStruct((T, DSA_WIDTH), BF16),
        scratch_shapes=[pltpu.VMEM((S, tq), F32), pltpu.VMEM((S + tk, tq), F32),
                        pltpu.VMEM((1, tq), F32),
                        pltpu.VMEM((DSA_HEADS, 1, tq), F32), pltpu.VMEM((DSA_HEADS, 1, tq), F32),
                        pltpu.VMEM((DSA_HEADS, DSA_HEAD_DIM + ONES_ROWS, tq), F32),
                        pltpu.VMEM((DSA_HEADS, 1, tq), F32),
                        pltpu.VMEM((DSA_HEADS, tk, tq), F32), pltpu.VMEM((DSA_HEADS, tk, tq), F32),
                        pltpu.VMEM((DSA_HEADS, tk, tq), BF16), pltpu.VMEM((DSA_HEADS, tk, tq), BF16)],
        compiler_params=_cparams(("parallel", "arbitrary")),
        name="dsa_attention",
    )(qt, k, vt, qit, ki, wt)


RW_PAIRS = RWKV_HEADS // 2
RW_GROUPS = RWKV_HEADS // 4


def _split3(x):
    hi = x.astype(BF16)
    r1 = x - hi.astype(F32)
    mid = r1.astype(BF16)
    lo = (r1 - mid.astype(F32)).astype(BF16)
    return hi, mid, lo


def _rwkv_kernel(p_ref, mu_ref, w0_ref, a0_ref, kk_ref, ka_ref, rk_ref, lnw_ref, lnb_ref,
                 wd_ref, wa_ref, wg_ref, y_ref, state_ref, prev_ref, *, C):
    c = pl.program_id(1)

    @pl.when(c == 0)
    def _():
        state_ref[...] = jnp.zeros_like(state_ref)
        prev_ref[...] = jnp.zeros_like(prev_ref)

    W = RWKV_WIDTH
    dot = functools.partial(jnp.dot, preferred_element_type=F32)

    p = p_ref[...]
    first_row = lax.broadcasted_iota(jnp.int32, p.shape, 0) == 0
    prev = jnp.where(first_row, prev_ref[...], pltpu.roll(p, 1, 0))
    prev_ref[...] = p[C - 1:C, :]
    xs = p + (prev - p) * mu_ref[...]

    small = xs[:, RWKV_SMALL:RWKV_SMALL + LANES]
    gate_in = xs[:, RWKV_SMALL + LANES:RWKV_SMALL + 3 * LANES]
    z = w0_ref[...] + dot(jnp.tanh(small).astype(BF16), wd_ref[...])
    softplus = jnp.maximum(-z, 0.0) + jnp.log(1.0 + jnp.exp(-jnp.abs(z)))
    logdec = -jnp.exp(-softplus - 0.5)
    rate = jax.nn.sigmoid(a0_ref[...] + dot(small.astype(BF16), wa_ref[...]))
    gate = dot(jax.nn.sigmoid(gate_in).astype(BF16), wg_ref[...])

    ti = lax.broadcasted_iota(jnp.int32, (C, C), 0)
    tj = lax.broadcasted_iota(jnp.int32, (C, C), 1)
    lower = jnp.where(tj <= ti, 1.0, 0.0).astype(BF16)
    hi, mid, lo = _split3(logdec)
    cum = dot(lower, hi) + dot(lower, mid) + dot(lower, lo)

    gi = lax.broadcasted_iota(jnp.int32, (LANES, LANES), 0) // RWKV_HEAD_DIM
    gj = lax.broadcasted_iota(jnp.int32, (LANES, LANES), 1) // RWKV_HEAD_DIM
    same_head = jnp.where(gi == gj, 1.0, 0.0).astype(BF16)

    def head_sum(parts):
        x = jnp.concatenate(parts, axis=0)
        xh = x.astype(BF16)
        xl = (x - xh.astype(F32)).astype(BF16)
        s = dot(xh, same_head) + dot(xl, same_head)
        return [s[n * C:(n + 1) * C] for n in range(len(parts))]

    lane = lax.broadcasted_iota(jnp.int32, (C, LANES), 1)
    even = lane < RWKV_HEAD_DIM
    zeros = jnp.zeros((C, LANES), F32)

    pairs = []
    for pi in range(RW_PAIRS):
        sl = slice(pi * LANES, (pi + 1) * LANES)
        r = xs[:, sl]
        k = xs[:, W + pi * LANES:W + (pi + 1) * LANES]
        v = xs[:, 2 * W + pi * LANES:2 * W + (pi + 1) * LANES]
        a = rate[:, sl]
        kk = k * kk_ref[:, sl]
        k2 = k * (1.0 + (a - 1.0) * ka_ref[:, sl])
        pairs.append(dict(r=r, v=v, a=a, kk=kk, k2=k2, sl=sl))
    sums = head_sum([d["kk"] * d["kk"] for d in pairs] + [d["r"] * d["k2"] * rk_ref[:, d["sl"]] for d in pairs])
    for pi, d in enumerate(pairs):
        sl = d["sl"]
        kkn = d["kk"] / jnp.maximum(jnp.sqrt(sums[pi]), 1e-12)
        d["bonus"] = sums[RW_PAIRS + pi] * d["v"]
        b = kkn * d["a"]
        cm = cum[:, sl]
        last = cm[C - 1:C, :]
        e_neg = jnp.exp(-cm)
        e_rem = jnp.exp(last - cm)
        d["At"] = -kkn * jnp.exp(cm - logdec[:, sl])
        d["Rt"] = d["r"] * jnp.exp(cm)
        d["Bt"] = b * e_neg
        d["Kt"] = d["k2"] * e_neg
        d["Bh"] = b * e_rem
        d["Kh"] = d["k2"] * e_rem
        d["wlast"] = jnp.exp(last)

    G = 4 * C
    ri = lax.broadcasted_iota(jnp.int32, (G, G), 0)
    ci = lax.broadcasted_iota(jnp.int32, (G, G), 1)
    same_block = (ri // C) == (ci // C)
    strict = jnp.logical_and(same_block, ci < ri)
    incl = jnp.logical_and(same_block, ci <= ri)
    pair_block = jnp.where(gi == gj, 1.0, 0.0)

    def masked4(x0, x1):
        return jnp.concatenate([
            jnp.concatenate([jnp.where(even, x0, 0.0), zeros], axis=1),
            jnp.concatenate([jnp.where(even, 0.0, x0), zeros], axis=1),
            jnp.concatenate([zeros, jnp.where(even, x1, 0.0)], axis=1),
            jnp.concatenate([zeros, jnp.where(even, 0.0, x1)], axis=1)], axis=0).astype(BF16)

    def plain4(x0, x1):
        return jnp.concatenate([
            jnp.concatenate([x0, zeros], axis=1), jnp.concatenate([x0, zeros], axis=1),
            jnp.concatenate([zeros, x1], axis=1), jnp.concatenate([zeros, x1], axis=1)],
            axis=0).astype(BF16)

    groups = []
    for g in range(RW_GROUPS):
        d0, d1 = pairs[2 * g], pairs[2 * g + 1]
        lhs_a = masked4(d0["At"], d1["At"])
        lhs_r = masked4(d0["Rt"], d1["Rt"])
        rhs_b = plain4(d0["Bt"], d1["Bt"])
        rhs_k = plain4(d0["Kt"], d1["Kt"])
        groups.append(dict(
            d=(d0, d1),
            m=jnp.where(strict, _nt_dot(lhs_a, rhs_b), 0.0).astype(BF16),
            m_ak=jnp.where(strict, _nt_dot(lhs_a, rhs_k), 0.0).astype(BF16),
            r_b=jnp.where(incl, _nt_dot(lhs_r, rhs_b), 0.0).astype(BF16),
            r_k=jnp.where(incl, _nt_dot(lhs_r, rhs_k), 0.0).astype(BF16),
            v4=jnp.concatenate([d0["v"], d0["v"], d1["v"], d1["v"]], axis=0).astype(BF16)))
    for g, grp in enumerate(groups):
        a0s, y0s = [], []
        for n, d in enumerate(grp["d"]):
            st = state_ref[2 * g + n].astype(BF16)
            both = _nt_dot(jnp.concatenate([d["At"], d["Rt"]], axis=0).astype(BF16), st)
            a0s.append(both[:C])
            y0s.append(both[C:])
        grp["y0"] = y0s
        grp["x"] = (jnp.concatenate([a0s[0], a0s[0], a0s[1], a0s[1]], axis=0)
                    + dot(grp["m_ak"], grp["v4"]))
    steps = max(1, int(np.ceil(np.log2(C))))
    for s in range(steps):
        for grp in groups:
            grp["x"] = grp["x"] + dot(grp["m"], grp["x"].astype(BF16))
        if s + 1 < steps:
            for grp in groups:
                grp["m"] = dot(grp["m"], grp["m"]).astype(BF16)
    ys = []
    for g, grp in enumerate(groups):
        x = grp["x"]
        yc = dot(grp["r_b"], x.astype(BF16)) + dot(grp["r_k"], grp["v4"])
        for n, d in enumerate(grp["d"]):
            lo_, hi_ = 2 * n * C, (2 * n + 1) * C
            y = grp["y0"][n] + jnp.where(even, yc[lo_:hi_], yc[hi_:hi_ + C])
            u = jnp.where(even, x[lo_:hi_], x[hi_:hi_ + C])
            uv = jnp.concatenate([u, d["v"]], axis=0).astype(BF16)
            bk = jnp.concatenate([d["Bh"], d["Kh"]], axis=0).astype(BF16)
            upd = lax.dot_general(uv, bk, (((0,), (0,)), ((), ())), preferred_element_type=F32)
            state_ref[2 * g + n] = state_ref[2 * g + n] * d["wlast"] + upd * pair_block
            ys.append(y)

    means = head_sum(ys)
    devs = [y - m_ * (1.0 / RWKV_HEAD_DIM) for y, m_ in zip(ys, means)]
    varis = head_sum([dv * dv for dv in devs])
    for pi, d in enumerate(pairs):
        sl = d["sl"]
        yn = devs[pi] * lax.rsqrt(varis[pi] * (1.0 / RWKV_HEAD_DIM) + GN_EPS)
        yn = yn * lnw_ref[:, sl] + lnb_ref[:, sl]
        y_ref[:, sl] = ((yn + d["bonus"]) * gate[:, sl]).astype(y_ref.dtype)


def rwkv_mixer(p_rwkv, mu, w0, a0, k_k, k_a, r_k, ln_w, ln_b, wd, wa, wg, B, S, C):
    T = B * S
    nc = S // C
    vec = pl.BlockSpec((1, RWKV_WIDTH), lambda b, c: (0, 0))
    return pl.pallas_call(
        functools.partial(_rwkv_kernel, C=C),
        grid=(B, nc),
        in_specs=[
            pl.BlockSpec((C, RWKV_PAD), lambda b, c: (b * nc + c, 0)),
            pl.BlockSpec((1, RWKV_PAD), lambda b, c: (0, 0)),
            vec, vec, vec, vec, vec, vec, vec,
            pl.BlockSpec((LANES, RWKV_WIDTH), lambda b, c: (0, 0)),
            pl.BlockSpec((LANES, RWKV_WIDTH), lambda b, c: (0, 0)),
            pl.BlockSpec((2 * LANES, RWKV_WIDTH), lambda b, c: (0, 0)),
        ],
        out_specs=pl.BlockSpec((C, RWKV_WIDTH), lambda b, c: (b * nc + c, 0)),
        out_shape=jax.ShapeDtypeStruct((T, RWKV_WIDTH), BF16),
        scratch_shapes=[pltpu.VMEM((RW_PAIRS, LANES, LANES), F32),
                        pltpu.VMEM((1, RWKV_PAD), F32)],
        compiler_params=_cparams(("parallel", "arbitrary")),
        name="rwkv_mixer",
    )(p_rwkv, mu, w0, a0, k_k, k_a, r_k, ln_w, ln_b, wd, wa, wg)


def _out_proj_kernel(x_ref, ya_ref, yb_ref, wa_ref, wb_ref, o_ref):
    o_ref[...] = (x_ref[...]
                  + jnp.dot(ya_ref[...], wa_ref[...], preferred_element_type=F32)
                  + jnp.dot(yb_ref[...], wb_ref[...], preferred_element_type=F32))


def out_proj(x, ya, yb, w_out, tm):
    T, D = x.shape
    half = ya.shape[1]
    row = lambda i: (i, 0)
    return pl.pallas_call(
        _out_proj_kernel,
        grid=(T // tm,),
        in_specs=[
            pl.BlockSpec((tm, D), row),
            pl.BlockSpec((tm, half), row),
            pl.BlockSpec((tm, half), row),
            pl.BlockSpec((half, D), lambda i: (0, 0)),
            pl.BlockSpec((half, D), lambda i: (1, 0)),
        ],
        out_specs=pl.BlockSpec((tm, D), row),
        out_shape=jax.ShapeDtypeStruct((T, D), F32),
        compiler_params=_cparams(("parallel",)),
        name="out_proj",
    )(x, ya, yb, w_out, w_out)


ROUTE_EXPERT0 = N_GROUPS


def _route(logits):
    lane = lax.broadcasted_iota(jnp.int32, logits.shape, 1)
    big = jnp.int32(LANES)
    is_grp = lane < N_GROUPS
    gl = jnp.where(is_grp, logits, NEG_INF)
    gmax = jnp.max(gl, axis=1, keepdims=True)
    p_grp = 1.0 / jnp.sum(jnp.exp(gl - gmax), axis=1, keepdims=True)
    g_sel = jnp.min(jnp.where(gl == gmax, lane, big), axis=1, keepdims=True)
    lo = ROUTE_EXPERT0 + g_sel * EXPERTS_PER_GROUP
    in_grp = jnp.logical_and(lane >= lo, lane < lo + EXPERTS_PER_GROUP)
    el = jnp.where(in_grp, logits, NEG_INF)
    ee = jnp.exp(el - jnp.max(el, axis=1, keepdims=True))
    pe = jnp.where(in_grp, ee / jnp.sum(ee, axis=1, keepdims=True), -1.0)
    p1 = jnp.max(pe, axis=1, keepdims=True)
    i1 = jnp.min(jnp.where(pe == p1, lane, big), axis=1, keepdims=True)
    pe2 = jnp.where(lane == i1, -1.0, pe)
    p2 = jnp.max(pe2, axis=1, keepdims=True)
    i2 = jnp.min(jnp.where(pe2 == p2, lane, big), axis=1, keepdims=True)
    tot = p1 + p2
    comb = jnp.where(lane == i1, p_grp * (p1 / tot), jnp.where(lane == i2, p_grp * (p2 / tot), 0.0))
    return comb, g_sel


MOE_TM = 1024
MOE_SUB = 256
MOE_PAD = (MOE_TM + N_GROUPS * (MOE_SUB - 1)) // MOE_SUB * MOE_SUB


def _moe_kernel(x_ref, g_ref, wr_hi_ref, wr_lo_ref, br_ref, wg_ref, wu_ref, wd_ref, o_ref,
                hs_ref, ys_ref, cs_ref, pos_ref, meta_ref):
    e = pl.program_id(1)
    tm = x_ref.shape[0]
    dot = functools.partial(jnp.dot, preferred_element_type=F32)

    @pl.when(e == 0)
    def _():
        x = x_ref[...]
        ms = jnp.mean(x * x, axis=-1, keepdims=True)
        h = x * lax.rsqrt(ms + NORM_EPS) * g_ref[...]
        hi = h.astype(BF16)
        lo = (h - hi.astype(F32)).astype(BF16)
        logits = dot(hi, wr_hi_ref[...]) + dot(lo, wr_hi_ref[...]) + dot(hi, wr_lo_ref[...])
        comb, g_sel = _route(logits + br_ref[...])

        lane = lax.broadcasted_iota(jnp.int32, (tm, LANES), 1)
        onehot = jnp.where(lane == g_sel, 1.0, 0.0)
        onehot_b = onehot.astype(BF16)
        ranks = []
        for c0 in range(0, tm, MOE_SUB):
            t_i = c0 + lax.broadcasted_iota(jnp.int32, (MOE_SUB, tm), 0)
            t_j = lax.broadcasted_iota(jnp.int32, (MOE_SUB, tm), 1)
            ranks.append(dot(jnp.where(t_j < t_i, 1.0, 0.0).astype(BF16), onehot_b))
        rank = jnp.concatenate(ranks, axis=0)
        counts = jnp.sum(onehot, axis=0, keepdims=True)
        padded = jnp.floor((counts + (MOE_SUB - 1.0)) * (1.0 / MOE_SUB)) * MOE_SUB
        lane1 = lax.broadcasted_iota(jnp.int32, (1, LANES), 1)
        base = jnp.zeros((1, LANES), F32)
        start = jnp.zeros((1, 1), F32)
        for g in range(N_GROUPS):
            pad_g = jnp.sum(jnp.where(lane1 == g, padded, 0.0), axis=1, keepdims=True)
            base = jnp.where(lane1 == g, start, base)
            meta_ref[g] = (start[0, 0] * (1.0 / MOE_SUB)).astype(jnp.int32)
            meta_ref[N_GROUPS + g] = (pad_g[0, 0] * (1.0 / MOE_SUB)).astype(jnp.int32)
            start = start + pad_g
        pos = jnp.sum(onehot * (rank + base), axis=1, keepdims=True)
        pos_b = jnp.broadcast_to(pos, (tm, LANES))
        pos_ref[...] = pos_b
        pos_row = pos_b.T[0:1, :]
        c_hi, c_mid, c_lo = _split3(comb)
        for c0 in range(0, MOE_PAD, MOE_SUB):
            r_i = (c0 + lax.broadcasted_iota(jnp.int32, (MOE_SUB, tm), 0)).astype(F32)
            perm = jnp.where(r_i == pos_row, 1.0, 0.0).astype(BF16)
            hs_ref[c0:c0 + MOE_SUB, :] = dot(perm, hi).astype(BF16)
            cs_ref[c0:c0 + MOE_SUB, :] = dot(perm, c_hi) + dot(perm, c_mid) + dot(perm, c_lo)
        ys_ref[...] = jnp.zeros(ys_ref.shape, BF16)
        o_ref[...] = x

    grp = e // EXPERTS_PER_GROUP
    first = meta_ref[grp]
    lane_s = lax.broadcasted_iota(jnp.int32, (MOE_SUB, LANES), 1)

    def sub_tile(si, carry):
        rows = pl.ds(pl.multiple_of((first + si) * MOE_SUB, MOE_SUB), MOE_SUB)
        hb = hs_ref[rows, :]
        hg = dot(hb, wg_ref[...])
        hu = dot(hb, wu_ref[...])
        he = (hg * jax.nn.sigmoid(hg) * hu).astype(BF16)
        ce = jnp.sum(jnp.where(lane_s == e + ROUTE_EXPERT0, cs_ref[rows, :], 0.0), axis=1, keepdims=True)
        ys_ref[rows, :] = (ys_ref[rows, :].astype(F32) + ce * dot(he, wd_ref[...])).astype(BF16)
        return carry

    lax.fori_loop(0, meta_ref[N_GROUPS + grp], sub_tile, 0)

    @pl.when(e == pl.num_programs(1) - 1)
    def _():
        c_i = lax.broadcasted_iota(jnp.int32, (MOE_SUB, MOE_PAD), 1).astype(F32)
        for c0 in range(0, tm, MOE_SUB):
            unperm = jnp.where(pos_ref[c0:c0 + MOE_SUB, 0:1] == c_i, 1.0, 0.0).astype(BF16)
            o_ref[c0:c0 + MOE_SUB, :] += dot(unperm, ys_ref[...])


def moe_block(x, g, wr_hi, wr_lo, br, w_gate, w_up, w_down):
    T, D = x.shape
    E, _, Fd = w_gate.shape
    tm = MOE_TM
    row = lambda i, e: (i, 0)
    fixed = lambda i, e: (0, 0)
    return pl.pallas_call(
        _moe_kernel,
        grid=(T // tm, E),
        in_specs=[
            pl.BlockSpec((tm, D), row, pipeline_mode=pl.Buffered(1)),
            pl.BlockSpec((1, D), fixed),
            pl.BlockSpec((D, LANES), fixed),
            pl.BlockSpec((D, LANES), fixed),
            pl.BlockSpec((1, LANES), fixed),
            pl.BlockSpec((None, D, Fd), lambda i, e: (e, 0, 0)),
            pl.BlockSpec((None, D, Fd), lambda i, e: (e, 0, 0)),
            pl.BlockSpec((None, Fd, D), lambda i, e: (e, 0, 0)),
        ],
        out_specs=pl.BlockSpec((tm, D), row, pipeline_mode=pl.Buffered(1)),
        out_shape=jax.ShapeDtypeStruct((T, D), F32),
        scratch_shapes=[pltpu.VMEM((MOE_PAD, D), BF16), pltpu.VMEM((MOE_PAD, D), BF16),
                        pltpu.VMEM((MOE_PAD, LANES), F32), pltpu.VMEM((tm, LANES), F32),
                        pltpu.SMEM((2 * N_GROUPS,), jnp.int32)],
        compiler_params=_cparams(("parallel", "arbitrary")),
        name="moe_block",
    )(x, g, wr_hi, wr_lo, br, w_gate, w_up, w_down)


def _prep_route(w_route_group, b_route_group, w_route_expert, b_route_expert):
    D = w_route_group.shape[0]
    pad = LANES - N_GROUPS - N_EXPERTS
    wr = jnp.concatenate([w_route_group, w_route_expert, jnp.zeros((D, pad), F32)], axis=1)
    br = jnp.concatenate([b_route_group, b_route_expert, jnp.zeros((pad,), F32)])[None]
    hi = wr.astype(BF16)
    lo = (wr - hi.astype(F32)).astype(BF16)
    return hi, lo, br


def _prep_lora(w_decay_up, w_aicl_up, w_gate_lora_up):
    zw = jnp.zeros((LANES - DECAY_LORA, RWKV_WIDTH), w_decay_up.dtype)
    wd = jnp.concatenate([w_decay_up, zw], axis=0)
    wa = jnp.concatenate([jnp.zeros((DECAY_LORA, RWKV_WIDTH), w_aicl_up.dtype), w_aicl_up], axis=0)
    wg = jnp.concatenate([w_gate_lora_up,
                          jnp.zeros((2 * LANES - GATE_LORA, RWKV_WIDTH), w_gate_lora_up.dtype)], axis=0)
    return wd.astype(BF16), wa.astype(BF16), wg.astype(BF16)


DSA_COLS = 4176
IN_PROJ_TILES = 2
IN_PROJ_ROWS = 512
RWKV_SPLITS = (RWKV_WIDTH, DECAY_LORA, RWKV_WIDTH, RWKV_WIDTH, AICL_LORA, GATE_LORA)


def _rwkv_cols(t):
    cuts = np.cumsum(RWKV_SPLITS)[:-1].tolist()
    r, dw, k, v, da, dg = jnp.split(t, cuts, axis=-1)
    pad = jnp.zeros(t.shape[:-1] + (RWKV_PAD - sum(RWKV_SPLITS),), t.dtype)
    return jnp.concatenate([r, k, v, dw, da, dg, pad], axis=-1)


def _prep_in_weights(w_in):
    pad = jnp.zeros((w_in.shape[0], DSA_PAD - DSA_COLS), w_in.dtype)
    w_dsa = jnp.concatenate([w_in[:, :DSA_COLS], pad], axis=1)
    w_rwkv = _rwkv_cols(w_in[:, DSA_COLS:])

    def tiles(w):
        d, n = w.shape
        return w.astype(BF16).reshape(d, IN_PROJ_TILES, n // IN_PROJ_TILES).transpose(1, 0, 2)

    return tiles(w_dsa), tiles(w_rwkv)


def kernel(x, g_mix, w_in, rwkv_shift_mix, q_gain, k_gain, w0, w_decay_up, a0, w_aicl_up, w_gate_lora_up, k_k, k_a, r_k, ln_x_w, ln_x_b, w_out, g_ffn, w_route_group, b_route_group, w_route_expert, b_route_expert, w_e_gate, w_e_up, w_e_down):
    B, S, D = x.shape
    T = B * S
    depth = g_mix.shape[0]
    xf = x.reshape(T, D)
    vec = lambda t: t.reshape(1, -1)
    for l in range(depth):
        w_dsa, w_rwkv = _prep_in_weights(w_in[l])
        p_dsa = norm_matmul(xf, vec(g_mix[l]), w_dsa, IN_PROJ_ROWS)
        p_rwkv = norm_matmul(xf, vec(g_mix[l]), w_rwkv, IN_PROJ_ROWS)
        q, k, v, qi, ki, w = dsa_prep(p_dsa, vec(q_gain[l]), vec(k_gain[l]), B, S, 256)
        y_dsa = dsa_attention(q, k, v, qi, ki, w, B, S, 256, 256)
        wd, wa, wg = _prep_lora(w_decay_up[l], w_aicl_up[l], w_gate_lora_up[l])
        y_rwkv = rwkv_mixer(p_rwkv, _rwkv_cols(vec(rwkv_shift_mix[l])), vec(w0[l]), vec(a0[l]),
                            vec(k_k[l]), vec(k_a[l]), vec(r_k[l]), vec(ln_x_w[l]), vec(ln_x_b[l]),
                            wd, wa, wg, B, S, CHUNK)
        xf = out_proj(xf, y_dsa, y_rwkv, w_out[l].astype(BF16), 512)
        wr_hi, wr_lo, br = _prep_route(w_route_group[l], b_route_group[l],
                                       w_route_expert[l], b_route_expert[l])
        xf = moe_block(xf, vec(g_ffn[l]), wr_hi, wr_lo, br, w_e_gate[l].astype(BF16),
                       w_e_up[l].astype(BF16), w_e_down[l].astype(BF16))
    return xf.reshape(B, S, D)
```

```python
import functools

import jax
import jax.numpy as jnp
import numpy as np
from jax import lax
from jax.experimental import pallas as pl
from jax.experimental.pallas import tpu as pltpu

F32 = jnp.float32
BF16 = jnp.bfloat16

NORM_EPS = 1e-6
CHUNK = 64
DSA_HEADS = 8
DSA_HEAD_DIM = 128
DSA_WIDTH = DSA_HEADS * DSA_HEAD_DIM
IDX_HEADS = 16
IDX_DIM = 64
TOPK_MAX = 256
ROPE_THETA = 10000.0
RWKV_HEADS = 16
RWKV_HEAD_DIM = 64
RWKV_WIDTH = RWKV_HEADS * RWKV_HEAD_DIM
DECAY_LORA = 64
AICL_LORA = 64
GATE_LORA = 160
GN_EPS = 64e-5
N_GROUPS = 4
EXPERTS_PER_GROUP = 4
N_EXPERTS = 16
D_EXPERT = 512

LANES = 128
SUBLANES = 8
VMEM_LIMIT = 56 * 1024 * 1024

DSA_PAD = 4352
RWKV_PAD = 3584
DSA_SMALL = 4096
RWKV_SMALL = 3072

NEG_INF = float("-inf")
INT_MIN = -(2 ** 31)
LOG2E = 1.4426950408889634


def _cparams(sem):
    return pltpu.CompilerParams(dimension_semantics=sem, vmem_limit_bytes=VMEM_LIMIT)


def _norm_matmul_kernel(x_ref, g_ref, w_ref, o_ref, h_ref):
    @pl.when(pl.program_id(1) == 0)
    def _():
        x = x_ref[...]
        ms = jnp.mean(x * x, axis=-1, keepdims=True)
        h_ref[...] = (x * lax.rsqrt(ms + NORM_EPS) * g_ref[...]).astype(BF16)

    o_ref[...] = jnp.dot(h_ref[...], w_ref[...], preferred_element_type=F32)


def norm_matmul(x, g, w, tm):
    T, D = x.shape
    nt, _, tn = w.shape
    N = nt * tn
    return pl.pallas_call(
        _norm_matmul_kernel,
        grid=(T // tm, nt),
        in_specs=[
            pl.BlockSpec((tm, D), lambda i, j: (i, 0)),
            pl.BlockSpec((1, D), lambda i, j: (0, 0)),
            pl.BlockSpec((None, D, tn), lambda i, j: (j, 0, 0)),
        ],
        out_specs=pl.BlockSpec((tm, tn), lambda i, j: (i, j)),
        out_shape=jax.ShapeDtypeStruct((T, N), F32),
        scratch_shapes=[pltpu.VMEM((tm, D), BF16)],
        compiler_params=_cparams(("parallel", "arbitrary")),
        name="norm_matmul",
    )(x, g, w)


def _rope_tables(S, d):
    half = d // 2
    inv = ROPE_THETA ** (-jnp.arange(half, dtype=F32) * (2.0 / d))
    ang = jnp.arange(S, dtype=F32)[:, None] * inv[None, :]
    cos, sin = jnp.cos(ang), jnp.sin(ang)
    return jnp.concatenate([cos, cos], axis=-1), jnp.concatenate([-sin, sin], axis=-1)


def _dsa_prep_kernel(p_ref, qg_ref, kg_ref, c128_ref, s128_ref, c64_ref, s64_ref,
                     qt_ref, k_ref, vt_ref, qit_ref, ki_ref, wt_ref):
    c128, s128 = c128_ref[...], s128_ref[...]
    c64, s64 = c64_ref[...], s64_ref[...]
    tm = c128.shape[0]
    first_half = (lax.broadcasted_iota(jnp.int32, (tm, LANES), 1) % IDX_DIM) < IDX_DIM // 2

    def rope128(y):
        return y * c128 + pltpu.roll(y, DSA_HEAD_DIM // 2, 1) * s128

    def rope64(y):
        partner = jnp.where(first_half, pltpu.roll(y, LANES - IDX_DIM // 2, 1),
                            pltpu.roll(y, IDX_DIM // 2, 1))
        return y * c64 + partner * s64

    ones = jnp.ones((LANES, LANES), BF16)
    e_i = lax.broadcasted_iota(jnp.int32, (LANES, LANES), 0)
    e_j = lax.broadcasted_iota(jnp.int32, (LANES, LANES), 1)
    eye = jnp.where(e_i == e_j, 1.0, 0.0).astype(BF16)

    def normed(x, gain):
        sq = x * x
        hi = sq.astype(BF16)
        lo = (sq - hi.astype(F32)).astype(BF16)
        ms = (jnp.dot(hi, ones, preferred_element_type=F32)
              + jnp.dot(lo, ones, preferred_element_type=F32)) * (1.0 / DSA_HEAD_DIM)
        return x * lax.rsqrt(ms + NORM_EPS) * gain

    def transposed(y):
        return _nt_dot(eye, y.astype(BF16)).astype(BF16)

    scale = DSA_HEAD_DIM ** -0.5 * LOG2E
    for h in range(DSA_HEADS):
        lo = h * DSA_HEAD_DIM
        q = p_ref[:, lo:lo + DSA_HEAD_DIM]
        qt_ref[lo:lo + DSA_HEAD_DIM, :] = transposed(rope128(normed(q, qg_ref[...])) * scale)
        k = p_ref[:, DSA_WIDTH + lo:DSA_WIDTH + lo + DSA_HEAD_DIM]
        k_ref[:, lo:lo + DSA_HEAD_DIM] = rope128(normed(k, kg_ref[...])).astype(BF16)
        v = p_ref[:, 2 * DSA_WIDTH + lo:2 * DSA_WIDTH + lo + DSA_HEAD_DIM]
        vt_ref[lo:lo + DSA_HEAD_DIM, :] = transposed(v)
    for j in range(IDX_HEADS * IDX_DIM // LANES):
        lo = 3 * DSA_WIDTH + j * LANES
        qit_ref[j * LANES:(j + 1) * LANES, :] = transposed(rope64(p_ref[:, lo:lo + LANES]))
    small = p_ref[:, DSA_SMALL:DSA_SMALL + LANES]
    lane = lax.broadcasted_iota(jnp.int32, (tm, LANES), 1)
    ki = rope64(small)
    ki_ref[:, :LANES] = jnp.where(lane < IDX_DIM, ki, 0.0).astype(BF16)
    ki_ref[:, LANES:] = jnp.where(lane >= IDX_DIM, pltpu.roll(ki, IDX_DIM, 1), 0.0).astype(BF16)
    wt_ref[...] = (small * (IDX_HEADS ** -0.5 * IDX_DIM ** -0.5)).T


def dsa_prep(p_dsa, q_gain, k_gain, B, S, tm):
    T = B * S
    c128, s128 = _rope_tables(S, DSA_HEAD_DIM)
    c64, s64 = _rope_tables(S, IDX_DIM)
    c64 = jnp.concatenate([c64, c64], axis=-1)
    s64 = jnp.concatenate([s64, s64], axis=-1)
    nsb = S // tm
    row = lambda i: (i, 0)
    pos = lambda i: (i % nsb, 0)
    fixed = lambda i: (0, 0)
    tile_t = lambda i: (i // nsb, i % nsb, 0, 0)
    wide_t = pl.BlockSpec((None, None, DSA_WIDTH, tm), tile_t)
    return pl.pallas_call(
        _dsa_prep_kernel,
        grid=(T // tm,),
        in_specs=[
            pl.BlockSpec((tm, DSA_PAD), row),
            pl.BlockSpec((1, DSA_HEAD_DIM), fixed),
            pl.BlockSpec((1, DSA_HEAD_DIM), fixed),
            pl.BlockSpec((tm, LANES), pos),
            pl.BlockSpec((tm, LANES), pos),
            pl.BlockSpec((tm, LANES), pos),
            pl.BlockSpec((tm, LANES), pos),
        ],
        out_specs=[
            wide_t,
            pl.BlockSpec((tm, DSA_WIDTH), row),
            wide_t,
            wide_t,
            pl.BlockSpec((tm, 2 * LANES), row),
            pl.BlockSpec((None, None, LANES, tm), tile_t),
        ],
        out_shape=[
            jax.ShapeDtypeStruct((B, nsb, DSA_WIDTH, tm), BF16),
            jax.ShapeDtypeStruct((T, DSA_WIDTH), BF16),
            jax.ShapeDtypeStruct((B, nsb, DSA_WIDTH, tm), BF16),
            jax.ShapeDtypeStruct((B, nsb, IDX_HEADS * IDX_DIM, tm), BF16),
            jax.ShapeDtypeStruct((T, 2 * LANES), BF16),
            jax.ShapeDtypeStruct((B, nsb, LANES, tm), F32),
        ],
        compiler_params=_cparams(("parallel",)),
        name="dsa_prep",
    )(p_dsa, q_gain, k_gain, c128, s128, c64, s64)


SCORE_ROWS = 128
DSA_TILE = 256
SEARCH_BLIND = 10
SEARCH_CAP = 40
PEEL_FROM = 16
INF = float("inf")
SHIFT_BOUND_MAX = 50.0
ONES_ROWS = 16


def _nt_dot(a, b):
    return lax.dot_general(a, b, (((1,), (1,)), ((), ())), preferred_element_type=F32)


def _dsa_kernel(qt_ref, k_ref, vt_ref, qit_ref, ki_ref, wt_ref, y_ref,
                sc_ref, bias_ref, thr_ref, m_ref, l_ref, acc_ref, ksq_ref,
                s0_ref, s1_ref, p0_ref, p1_ref, *, tq, tk, top_k):
    i = pl.program_id(1)
    nkb = ((i + 1) * tq + tk - 1) // tk
    qpos = i * tq + lax.broadcasted_iota(jnp.int32, (1, tq), 1)
    limit = (qpos // CHUNK + 1) * CHUNK
    kf = float(top_k)

    def blk(kb):
        return pl.ds(pl.multiple_of(kb * tk, tk), tk)

    def key_pos(kb, rows, r0=0):
        return kb * tk + r0 + lax.broadcasted_iota(jnp.int32, (rows, tq), 0)

    def score_block(kb, carry):
        mn, mx = carry
        for r0 in range(0, tk, SCORE_ROWS):
            kblk = ki_ref[pl.ds(pl.multiple_of(kb * tk, tk) + r0, SCORE_ROWS), :]
            acc = jnp.zeros((SCORE_ROWS, tq), F32)
            for j in range(IDX_HEADS // 2):
                qblk = qit_ref[j * LANES:(j + 1) * LANES, :]
                s_even = jnp.dot(kblk[:, :LANES], qblk, preferred_element_type=F32)
                s_odd = jnp.dot(kblk[:, LANES:], qblk, preferred_element_type=F32)
                acc = acc + jnp.maximum(s_even, 0.0) * wt_ref[IDX_DIM + 2 * j:IDX_DIM + 2 * j + 1, :]
                acc = acc + jnp.maximum(s_odd, 0.0) * wt_ref[IDX_DIM + 2 * j + 1:IDX_DIM + 2 * j + 2, :]
            adm = key_pos(kb, SCORE_ROWS, r0) < limit
            sc_ref[pl.ds(pl.multiple_of(kb * tk, tk) + r0, SCORE_ROWS), :] = jnp.where(adm, acc, NEG_INF)
            mn = jnp.minimum(mn, jnp.min(jnp.where(adm, acc, INF), axis=0, keepdims=True))
            mx = jnp.maximum(mx, jnp.max(jnp.where(adm, acc, NEG_INF), axis=0, keepdims=True))
        return mn, mx

    row_min, row_max = lax.fori_loop(
        0, nkb, score_block, (jnp.full((1, tq), INF, F32), jnp.full((1, tq), NEG_INF, F32)))

    def fold8(x, op):
        return op(x.reshape(x.shape[0] // SUBLANES, SUBLANES, tq), axis=0)

    def count_where(pred):
        def body(kb, acc):
            return acc + fold8(jnp.where(pred(sc_ref[blk(kb), :]), 1.0, 0.0), jnp.sum)
        part = lax.fori_loop(0, nkb, body, jnp.zeros((SUBLANES, tq), F32))
        return jnp.sum(part, axis=0, keepdims=True)

    def count_ge(x):
        return count_where(lambda s: s >= x)

    def max_below(x):
        def body(kb, acc):
            s = sc_ref[blk(kb), :]
            return jnp.maximum(acc, fold8(jnp.where(s < x, s, NEG_INF), jnp.max))
        part = lax.fori_loop(0, nkb, body, jnp.full((SUBLANES, tq), NEG_INF, F32))
        return jnp.max(part, axis=0, keepdims=True)

    take_all = limit <= top_k

    def search_cond(st):
        it, done = st[0], st[1]
        return jnp.logical_and(it < SEARCH_CAP, jnp.min(done) < 0.5)

    def search_step(st, may_peel=True):
        it, done, thr, lo, hi = st
        mid = 0.5 * lo + 0.5 * hi
        peel = jnp.logical_and(it >= PEEL_FROM, it % 4 == 3) if may_peel else False
        if may_peel:
            mid = lax.cond(peel, lambda: max_below(hi), lambda: mid)
        c = count_ge(mid)
        active = done < 0.5
        hit = jnp.logical_or(c == kf, jnp.logical_and(peel, c >= kf))
        fin = jnp.logical_and(active, hit)
        up = jnp.logical_and(active, jnp.logical_and(jnp.logical_not(hit), c > kf))
        dn = jnp.logical_and(active, c < kf)
        return (it + 1, jnp.where(fin, 1.0, done), jnp.where(fin, mid, thr),
                jnp.where(up, mid, lo), jnp.where(dn, mid, hi))

    done0 = jnp.where(take_all, 1.0, 0.0)
    st = (jnp.int32(0), done0, jnp.full((1, tq), NEG_INF, F32), row_min,
          row_max + (jnp.abs(row_max) * 2.0 ** -10 + 1e-30))
    st = lax.fori_loop(0, SEARCH_BLIND, lambda _, s: search_step(s, may_peel=False), st)
    st = lax.while_loop(search_cond, search_step, st)
    done, thr = st[1], st[2]
    thr_ref[...] = thr

    @pl.when(jnp.min(done) < 0.5)
    def _():
        def from_key(key):
            return pltpu.bitcast(jnp.where(key < 0, key ^ jnp.int32(0x7FFFFFFF), key), F32)

        def bit_step(b, key):
            cand = key ^ jnp.left_shift(jnp.int32(1), 31 - b)
            return jnp.where(count_ge(from_key(cand)) >= kf, cand, key)

        key = lax.fori_loop(0, 32, bit_step, jnp.full((1, tq), INT_MIN, jnp.int32))
        thr_ref[...] = jnp.where(done < 0.5, from_key(key), thr)

    thr = thr_ref[...]
    tied = jnp.logical_and(count_ge(thr) > kf, jnp.logical_not(take_all))
    any_tied = jnp.max(jnp.where(tied, 1.0, 0.0)) > 0.5

    @pl.when(jnp.logical_not(any_tied))
    def _():
        def body(kb, carry):
            sel = jnp.logical_and(sc_ref[blk(kb), :] >= thr, key_pos(kb, tk) < limit)
            bias_ref[blk(kb), :] = jnp.where(sel, 0.0, NEG_INF)
            return carry
        lax.fori_loop(0, nkb, body, 0)

    @pl.when(any_tied)
    def _():
        need = kf - count_where(lambda s: s > thr)
        r_i = lax.broadcasted_iota(jnp.int32, (tk, tk), 0)
        c_i = lax.broadcasted_iota(jnp.int32, (tk, tk), 1)
        lower = jnp.where(c_i <= r_i, 1.0, 0.0).astype(BF16)

        def body(kb, seen):
            s = sc_ref[blk(kb), :]
            eq = s == thr
            prefix = seen + jnp.dot(lower, jnp.where(eq, 1.0, 0.0).astype(BF16),
                                    preferred_element_type=F32)
            sel = jnp.logical_or(s > thr, jnp.logical_and(eq, prefix <= need))
            sel = jnp.logical_or(sel, take_all)
            sel = jnp.logical_and(sel, key_pos(kb, tk) < limit)
            bias_ref[blk(kb), :] = jnp.where(sel, 0.0, NEG_INF)
            return prefix[tk - 1:tk, :]
        lax.fori_loop(0, nkb, body, jnp.zeros((1, tq), F32))

    heads = [slice(h * DSA_HEAD_DIM, (h + 1) * DSA_HEAD_DIM) for h in range(DSA_HEADS)]

    @pl.when(i == 0)
    def _():
        def body(kb, carry):
            out = []
            for h, hs in enumerate(heads):
                kh = k_ref[blk(kb), hs].astype(F32)
                sq = jnp.max(jnp.sum(kh * kh, axis=1, keepdims=True), axis=0, keepdims=True)
                out.append(jnp.maximum(carry[h], sq))
            return tuple(out)
        ksq = lax.fori_loop(0, k_ref.shape[0] // tk, body,
                            tuple(jnp.zeros((1, 1), F32) for _ in heads))
        for h in range(DSA_HEADS):
            ksq_ref[h] = jnp.broadcast_to(ksq[h], (1, tq))

    worst = jnp.zeros((1, tq), F32)
    for h, hs in enumerate(heads):
        qh = qt_ref[hs, :].astype(F32)
        bound = jnp.sqrt(jnp.sum(qh * qh, axis=0, keepdims=True) * ksq_ref[h])
        m_ref[h] = bound
        worst = jnp.maximum(worst, bound)
    shift_ok = jnp.max(worst) <= SHIFT_BOUND_MAX
    acc_ref[...] = jnp.zeros(acc_ref.shape, F32)

    @pl.when(shift_ok)
    def _():
        ones = jnp.ones((ONES_ROWS, tk), BF16)
        n_blocks = k_ref.shape[0] // tk
        bias_ref[pl.ds(n_blocks * tk, tk), :] = jnp.full((tk, tq), NEG_INF, F32)
        s1_ref[...] = jnp.zeros(s1_ref.shape, F32)
        p0_ref[...] = jnp.zeros(p0_ref.shape, BF16)

        def step(t, s_w, s_r, p_w, p_r):
            ta = jnp.minimum(t, nkb - 1)
            for h, hs in enumerate(heads):
                s_w[h] = jnp.dot(k_ref[blk(ta), hs], qt_ref[hs, :], preferred_element_type=F32)
            tb = t - 1
            valid = jnp.logical_and(tb >= 0, tb < nkb)
            bias = bias_ref[blk(jnp.where(valid, tb, n_blocks)), :]
            for h in range(DSA_HEADS):
                p_w[h] = jnp.exp2(s_r[h] + bias - m_ref[h]).astype(BF16)
            tc = jnp.maximum(t - 2, 0)
            for h, hs in enumerate(heads):
                acc_ref[h] += jnp.dot(jnp.concatenate([vt_ref[tc, hs, :], ones], axis=0), p_r[h],
                                      preferred_element_type=F32)

        def one_step(t, carry):
            @pl.when(t % 2 == 0)
            def _():
                step(t, s0_ref, s1_ref, p1_ref, p0_ref)

            @pl.when(t % 2 == 1)
            def _():
                step(t, s1_ref, s0_ref, p0_ref, p1_ref)
            return carry

        lax.fori_loop(0, nkb + 2, one_step, 0)
        for h, hs in enumerate(heads):
            acc = acc_ref[h]
            y = acc[:DSA_HEAD_DIM] / acc[DSA_HEAD_DIM:DSA_HEAD_DIM + 1]
            y_ref[:, hs] = y.T.astype(y_ref.dtype)

    @pl.when(jnp.logical_not(shift_ok))
    def _():
        m_ref[...] = jnp.full(m_ref.shape, NEG_INF, F32)
        l_ref[...] = jnp.zeros(l_ref.shape, F32)

        def attend(kb, carry):
            bias = bias_ref[blk(kb), :]
            for h, hs in enumerate(heads):
                s = jnp.dot(k_ref[blk(kb), hs], qt_ref[hs, :], preferred_element_type=F32) + bias
                m = m_ref[h]
                m_new = jnp.maximum(m, jnp.max(s, axis=0, keepdims=True))
                m_use = jnp.where(m_new == NEG_INF, 0.0, m_new)
                p = jnp.exp2(s - m_use)
                alpha = jnp.exp2(m - m_use)
                m_ref[h] = m_new
                l_ref[h] = alpha * l_ref[h] + jnp.sum(p, axis=0, keepdims=True)
                acc_ref[h, :DSA_HEAD_DIM, :] = (
                    alpha * acc_ref[h, :DSA_HEAD_DIM, :]
                    + jnp.dot(vt_ref[kb, hs, :], p.astype(BF16), preferred_element_type=F32))
            return carry

        lax.fori_loop(0, nkb, attend, 0)
        for h, hs in enumerate(heads):
            y_ref[:, hs] = (acc_ref[h, :DSA_HEAD_DIM, :] / l_ref[h]).T.astype(y_ref.dtype)


def dsa_attention(qt, k, vt, qit, ki, wt, B, S, tq, tk):
    assert tq == tk and qt.shape[-1] == tq
    T = B * S
    nq = S // tq
    top_k = min(TOPK_MAX, S // 4)
    qtile = lambda b, i: (b, i, 0, 0)
    seq = lambda b, i: (b, 0)
    return pl.pallas_call(
        functools.partial(_dsa_kernel, tq=tq, tk=tk, top_k=top_k),
        grid=(B, nq),
        in_specs=[
            pl.BlockSpec((None, None, DSA_WIDTH, tq), qtile),
            pl.BlockSpec((S, DSA_WIDTH), seq),
            pl.BlockSpec((None, nq, DSA_WIDTH, tk), lambda b, i: (b, 0, 0, 0)),
            pl.BlockSpec((None, None, IDX_HEADS * IDX_DIM, tq), qtile),
            pl.BlockSpec((S, 2 * LANES), seq),
            pl.BlockSpec((None, None, LANES, tq), qtile),
        ],
        out_specs=pl.BlockSpec((tq, DSA_WIDTH), lambda b, i: (b * nq + i, 0)),
        out_shape=jax.ShapeDtypeStruct((T, DSA_WIDTH), BF16),
        scratch_shapes=[pltpu.VMEM((S, tq), F32), pltpu.VMEM((S + tk, tq), F32),
                        pltpu.VMEM((1, tq), F32),
                        pltpu.VMEM((DSA_HEADS, 1, tq), F32), pltpu.VMEM((DSA_HEADS, 1, tq), F32),
                        pltpu.VMEM((DSA_HEADS, DSA_HEAD_DIM + ONES_ROWS, tq), F32),
                        pltpu.VMEM((DSA_HEADS, 1, tq), F32),
                        pltpu.VMEM((DSA_HEADS, tk, tq), F32), pltpu.VMEM((DSA_HEADS, tk, tq), F32),
                        pltpu.VMEM((DSA_HEADS, tk, tq), BF16), pltpu.VMEM((DSA_HEADS, tk, tq), BF16)],
        compiler_params=_cparams(("parallel", "arbitrary")),
        name="dsa_attention",
    )(qt, k, vt, qit, ki, wt)


RW_PAIRS = RWKV_HEADS // 2
RW_GROUPS = RWKV_HEADS // 4


def _split3(x):
    hi = x.astype(BF16)
    r1 = x - hi.astype(F32)
    mid = r1.astype(BF16)
    lo = (r1 - mid.astype(F32)).astype(BF16)
    return hi, mid, lo


def _rwkv_kernel(p_ref, mu_ref, w0_ref, a0_ref, kk_ref, ka_ref, rk_ref, lnw_ref, lnb_ref,
                 wd_ref, wa_ref, wg_ref, y_ref, state_ref, prev_ref, *, C):
    c = pl.program_id(1)

    @pl.when(c == 0)
    def _():
        state_ref[...] = jnp.zeros_like(state_ref)
        prev_ref[...] = jnp.zeros_like(prev_ref)

    W = RWKV_WIDTH
    dot = functools.partial(jnp.dot, preferred_element_type=F32)

    p = p_ref[...]
    first_row = lax.broadcasted_iota(jnp.int32, p.shape, 0) == 0
    prev = jnp.where(first_row, prev_ref[...], pltpu.roll(p, 1, 0))
    prev_ref[...] = p[C - 1:C, :]
    xs = p + (prev - p) * mu_ref[...]

    small = xs[:, RWKV_SMALL:RWKV_SMALL + LANES]
    gate_in = xs[:, RWKV_SMALL + LANES:RWKV_SMALL + 3 * LANES]
    z = w0_ref[...] + dot(jnp.tanh(small).astype(BF16), wd_ref[...])
    softplus = jnp.maximum(-z, 0.0) + jnp.log(1.0 + jnp.exp(-jnp.abs(z)))
    logdec = -jnp.exp(-softplus - 0.5)
    rate = jax.nn.sigmoid(a0_ref[...] + dot(small.astype(BF16), wa_ref[...]))
    gate = dot(jax.nn.sigmoid(gate_in).astype(BF16), wg_ref[...])

    ti = lax.broadcasted_iota(jnp.int32, (C, C), 0)
    tj = lax.broadcasted_iota(jnp.int32, (C, C), 1)
    lower = jnp.where(tj <= ti, 1.0, 0.0).astype(BF16)
    hi, mid, lo = _split3(logdec)
    cum = dot(lower, hi) + dot(lower, mid) + dot(lower, lo)

    gi = lax.broadcasted_iota(jnp.int32, (LANES, LANES), 0) // RWKV_HEAD_DIM
    gj = lax.broadcasted_iota(jnp.int32, (LANES, LANES), 1) // RWKV_HEAD_DIM
    same_head = jnp.where(gi == gj, 1.0, 0.0).astype(BF16)

    def head_sum(parts):
        x = jnp.concatenate(parts, axis=0)
        xh = x.astype(BF16)
        xl = (x - xh.astype(F32)).astype(BF16)
        s = dot(xh, same_head) + dot(xl, same_head)
        return [s[n * C:(n + 1) * C] for n in range(len(parts))]

    lane = lax.broadcasted_iota(jnp.int32, (C, LANES), 1)
    even = lane < RWKV_HEAD_DIM
    zeros = jnp.zeros((C, LANES), F32)

    pairs = []
    for pi in range(RW_PAIRS):
        sl = slice(pi * LANES, (pi + 1) * LANES)
        r = xs[:, sl]
        k = xs[:, W + pi * LANES:W + (pi + 1) * LANES]
        v = xs[:, 2 * W + pi * LANES:2 * W + (pi + 1) * LANES]
        a = rate[:, sl]
        kk = k * kk_ref[:, sl]
        k2 = k * (1.0 + (a - 1.0) * ka_ref[:, sl])
        pairs.append(dict(r=r, v=v, a=a, kk=kk, k2=k2, sl=sl))
    sums = head_sum([d["kk"] * d["kk"] for d in pairs] + [d["r"] * d["k2"] * rk_ref[:, d["sl"]] for d in pairs])
    for pi, d in enumerate(pairs):
        sl = d["sl"]
        kkn = d["kk"] / jnp.maximum(jnp.sqrt(sums[pi]), 1e-12)
        d["bonus"] = sums[RW_PAIRS + pi] * d["v"]
        b = kkn * d["a"]
        cm = cum[:, sl]
        last = cm[C - 1:C, :]
        e_neg = jnp.exp(-cm)
        e_rem = jnp.exp(last - cm)
        d["At"] = -kkn * jnp.exp(cm - logdec[:, sl])
        d["Rt"] = d["r"] * jnp.exp(cm)
        d["Bt"] = b * e_neg
        d["Kt"] = d["k2"] * e_neg
        d["Bh"] = b * e_rem
        d["Kh"] = d["k2"] * e_rem
        d["wlast"] = jnp.exp(last)

    G = 4 * C
    ri = lax.broadcasted_iota(jnp.int32, (G, G), 0)
    ci = lax.broadcasted_iota(jnp.int32, (G, G), 1)
    same_block = (ri // C) == (ci // C)
    strict = jnp.logical_and(same_block, ci < ri)
    incl = jnp.logical_and(same_block, ci <= ri)
    pair_block = jnp.where(gi == gj, 1.0, 0.0)

    def masked4(x0, x1):
        return jnp.concatenate([
            jnp.concatenate([jnp.where(even, x0, 0.0), zeros], axis=1),
            jnp.concatenate([jnp.where(even, 0.0, x0), zeros], axis=1),
            jnp.concatenate([zeros, jnp.where(even, x1, 0.0)], axis=1),
            jnp.concatenate([zeros, jnp.where(even, 0.0, x1)], axis=1)], axis=0).astype(BF16)

    def plain4(x0, x1):
        return jnp.concatenate([
            jnp.concatenate([x0, zeros], axis=1), jnp.concatenate([x0, zeros], axis=1),
            jnp.concatenate([zeros, x1], axis=1), jnp.concatenate([zeros, x1], axis=1)],
            axis=0).astype(BF16)

    groups = []
    for g in range(RW_GROUPS):
        d0, d1 = pairs[2 * g], pairs[2 * g + 1]
        lhs_a = masked4(d0["At"], d1["At"])
        lhs_r = masked4(d0["Rt"], d1["Rt"])
        rhs_b = plain4(d0["Bt"], d1["Bt"])
        rhs_k = plain4(d0["Kt"], d1["Kt"])
        groups.append(dict(
            d=(d0, d1),
            m=jnp.where(strict, _nt_dot(lhs_a, rhs_b), 0.0).astype(BF16),
            m_ak=jnp.where(strict, _nt_dot(lhs_a, rhs_k), 0.0).astype(BF16),
            r_b=jnp.where(incl, _nt_dot(lhs_r, rhs_b), 0.0).astype(BF16),
            r_k=jnp.where(incl, _nt_dot(lhs_r, rhs_k), 0.0).astype(BF16),
            v4=jnp.concatenate([d0["v"], d0["v"], d1["v"], d1["v"]], axis=0).astype(BF16)))
    for g, grp in enumerate(groups):
        a0s, y0s = [], []
        for n, d in enumerate(grp["d"]):
            st = state_ref[2 * g + n].astype(BF16)
            both = _nt_dot(jnp.concatenate([d["At"], d["Rt"]], axis=0).astype(BF16), st)
            a0s.append(both[:C])
            y0s.append(both[C:])
        grp["y0"] = y0s
        grp["x"] = (jnp.concatenate([a0s[0], a0s[0], a0s[1], a0s[1]], axis=0)
                    + dot(grp["m_ak"], grp["v4"]))
    steps = max(1, int(np.ceil(np.log2(C))))
    for s in range(steps):
        for grp in groups:
            grp["x"] = grp["x"] + dot(grp["m"], grp["x"].astype(BF16))
        if s + 1 < steps:
            for grp in groups:
                grp["m"] = dot(grp["m"], grp["m"]).astype(BF16)
    ys = []
    for g, grp in enumerate(groups):
        x = grp["x"]
        yc = dot(grp["r_b"], x.astype(BF16)) + dot(grp["r_k"], grp["v4"])
        for n, d in enumerate(grp["d"]):
            lo_, hi_ = 2 * n * C, (2 * n + 1) * C
            y = grp["y0"][n] + jnp.where(even, yc[lo_:hi_], yc[hi_:hi_ + C])
            u = jnp.where(even, x[lo_:hi_], x[hi_:hi_ + C])
            uv = jnp.concatenate([u, d["v"]], axis=0).astype(BF16)
            bk = jnp.concatenate([d["Bh"], d["Kh"]], axis=0).astype(BF16)
            upd = lax.dot_general(uv, bk, (((0,), (0,)), ((), ())), preferred_element_type=F32)
            state_ref[2 * g + n] = state_ref[2 * g + n] * d["wlast"] + upd * pair_block
            ys.append(y)

    means = head_sum(ys)
    devs = [y - m_ * (1.0 / RWKV_HEAD_DIM) for y, m_ in zip(ys, means)]
    varis = head_sum([dv * dv for dv in devs])
    for pi, d in enumerate(pairs):
        sl = d["sl"]
        yn = devs[pi] * lax.rsqrt(varis[pi] * (1.0 / RWKV_HEAD_DIM) + GN_EPS)
        yn = yn * lnw_ref[:, sl] + lnb_ref[:, sl]
        y_ref[:, sl] = ((yn + d["bonus"]) * gate[:, sl]).astype(y_ref.dtype)


def rwkv_mixer(p_rwkv, mu, w0, a0, k_k, k_a, r_k, ln_w, ln_b, wd, wa, wg, B, S, C):
    T = B * S
    nc = S // C
    vec = pl.BlockSpec((1, RWKV_WIDTH), lambda b, c: (0, 0))
    return pl.pallas_call(
        functools.partial(_rwkv_kernel, C=C),
        grid=(B, nc),
        in_specs=[
            pl.BlockSpec((C, RWKV_PAD), lambda b, c: (b * nc + c, 0)),
            pl.BlockSpec((1, RWKV_PAD), lambda b, c: (0, 0)),
            vec, vec, vec, vec, vec, vec, vec,
            pl.BlockSpec((LANES, RWKV_WIDTH), lambda b, c: (0, 0)),
            pl.BlockSpec((LANES, RWKV_WIDTH), lambda b, c: (0, 0)),
            pl.BlockSpec((2 * LANES, RWKV_WIDTH), lambda b, c: (0, 0)),
        ],
        out_specs=pl.BlockSpec((C, RWKV_WIDTH), lambda b, c: (b * nc + c, 0)),
        out_shape=jax.ShapeDtypeStruct((T, RWKV_WIDTH), BF16),
        scratch_shapes=[pltpu.VMEM((RW_PAIRS, LANES, LANES), F32),
                        pltpu.VMEM((1, RWKV_PAD), F32)],
        compiler_params=_cparams(("parallel", "arbitrary")),
        name="rwkv_mixer",
    )(p_rwkv, mu, w0, a0, k_k, k_a, r_k, ln_w, ln_b, wd, wa, wg)


def _out_proj_kernel(x_ref, ya_ref, yb_ref, wa_ref, wb_ref, o_ref):
    o_ref[...] = (x_ref[...]
                  + jnp.dot(ya_ref[...], wa_ref[...], preferred_element_type=F32)
                  + jnp.dot(yb_ref[...], wb_ref[...], preferred_element_type=F32))


def out_proj(x, ya, yb, w_out, tm):
    T, D = x.shape
    half = ya.shape[1]
    row = lambda i: (i, 0)
    return pl.pallas_call(
        _out_proj_kernel,
        grid=(T // tm,),
        in_specs=[
            pl.BlockSpec((tm, D), row),
            pl.BlockSpec((tm, half), row),
            pl.BlockSpec((tm, half), row),
            pl.BlockSpec((half, D), lambda i: (0, 0)),
            pl.BlockSpec((half, D), lambda i: (1, 0)),
        ],
        out_specs=pl.BlockSpec((tm, D), row),
        out_shape=jax.ShapeDtypeStruct((T, D), F32),
        compiler_params=_cparams(("parallel",)),
        name="out_proj",
    )(x, ya, yb, w_out, w_out)


ROUTE_EXPERT0 = N_GROUPS


def _route(logits):
    lane = lax.broadcasted_iota(jnp.int32, logits.shape, 1)
    big = jnp.int32(LANES)
    is_grp = lane < N_GROUPS
    gl = jnp.where(is_grp, logits, NEG_INF)
    gmax = jnp.max(gl, axis=1, keepdims=True)
    p_grp = 1.0 / jnp.sum(jnp.exp(gl - gmax), axis=1, keepdims=True)
    g_sel = jnp.min(jnp.where(gl == gmax, lane, big), axis=1, keepdims=True)
    lo = ROUTE_EXPERT0 + g_sel * EXPERTS_PER_GROUP
    in_grp = jnp.logical_and(lane >= lo, lane < lo + EXPERTS_PER_GROUP)
    el = jnp.where(in_grp, logits, NEG_INF)
    ee = jnp.exp(el - jnp.max(el, axis=1, keepdims=True))
    pe = jnp.where(in_grp, ee / jnp.sum(ee, axis=1, keepdims=True), -1.0)
    p1 = jnp.max(pe, axis=1, keepdims=True)
    i1 = jnp.min(jnp.where(pe == p1, lane, big), axis=1, keepdims=True)
    pe2 = jnp.where(lane == i1, -1.0, pe)
    p2 = jnp.max(pe2, axis=1, keepdims=True)
    i2 = jnp.min(jnp.where(pe2 == p2, lane, big), axis=1, keepdims=True)
    tot = p1 + p2
    comb = jnp.where(lane == i1, p_grp * (p1 / tot), jnp.where(lane == i2, p_grp * (p2 / tot), 0.0))
    return comb, g_sel


MOE_TM = 1024
MOE_SUB = 288
MOE_CHUNK = 256
MOE_PAD = (MOE_TM + N_GROUPS * (MOE_SUB - 1)) // MOE_SUB * MOE_SUB
MOE_ROWS = -(-MOE_PAD // LANES) * LANES


def _moe_kernel(x_ref, g_ref, wr_hi_ref, wr_lo_ref, br_ref, wg_ref, wu_ref, wd_ref, o_ref,
                hs_ref, ys_ref, cs_ref, pos_ref, meta_ref):
    e = pl.program_id(1)
    tm = x_ref.shape[0]
    dot = functools.partial(jnp.dot, preferred_element_type=F32)

    @pl.when(e == 0)
    def _():
        x = x_ref[...]
        ms = jnp.mean(x * x, axis=-1, keepdims=True)
        h = x * lax.rsqrt(ms + NORM_EPS) * g_ref[...]
        hi = h.astype(BF16)
        lo = (h - hi.astype(F32)).astype(BF16)
        logits = dot(hi, wr_hi_ref[...]) + dot(lo, wr_hi_ref[...]) + dot(hi, wr_lo_ref[...])
        comb, g_sel = _route(logits + br_ref[...])

        lane = lax.broadcasted_iota(jnp.int32, (tm, LANES), 1)
        onehot = jnp.where(lane == g_sel, 1.0, 0.0)
        onehot_b = onehot.astype(BF16)
        ranks = []
        for c0 in range(0, tm, MOE_CHUNK):
            t_i = c0 + lax.broadcasted_iota(jnp.int32, (MOE_CHUNK, tm), 0)
            t_j = lax.broadcasted_iota(jnp.int32, (MOE_CHUNK, tm), 1)
            ranks.append(dot(jnp.where(t_j < t_i, 1.0, 0.0).astype(BF16), onehot_b))
        rank = jnp.concatenate(ranks, axis=0)
        counts = jnp.sum(onehot, axis=0, keepdims=True)
        padded = jnp.floor((counts + (MOE_SUB - 0.5)) * (1.0 / MOE_SUB)) * MOE_SUB
        lane1 = lax.broadcasted_iota(jnp.int32, (1, LANES), 1)
        base = jnp.zeros((1, LANES), F32)
        start = jnp.zeros((1, 1), F32)
        for g in range(N_GROUPS):
            pad_g = jnp.sum(jnp.where(lane1 == g, padded, 0.0), axis=1, keepdims=True)
            base = jnp.where(lane1 == g, start, base)
            meta_ref[g] = (start[0, 0] * (1.0 / MOE_SUB) + 0.5).astype(jnp.int32)
            meta_ref[N_GROUPS + g] = (pad_g[0, 0] * (1.0 / MOE_SUB) + 0.5).astype(jnp.int32)
            start = start + pad_g
        pos = jnp.sum(onehot * (rank + base), axis=1, keepdims=True)
        pos_b = jnp.broadcast_to(pos, (tm, LANES))
        pos_ref[...] = pos_b
        pos_row = pos_b.T[0:1, :]
        c_hi, c_mid, c_lo = _split3(comb)
        for c0 in range(0, MOE_PAD, MOE_SUB):
            r_i = (c0 + lax.broadcasted_iota(jnp.int32, (MOE_SUB, tm), 0)).astype(F32)
            perm = jnp.where(r_i == pos_row, 1.0, 0.0).astype(BF16)
            hs_ref[c0:c0 + MOE_SUB, :] = dot(perm, hi).astype(BF16)
            cs_ref[c0:c0 + MOE_SUB, :] = dot(perm, c_hi) + dot(perm, c_mid) + dot(perm, c_lo)
        ys_ref[...] = jnp.zeros(ys_ref.shape, BF16)
        o_ref[...] = x

    grp = e // EXPERTS_PER_GROUP
    first = meta_ref[grp]
    lane_s = lax.broadcasted_iota(jnp.int32, (MOE_SUB, LANES), 1)

    def sub_tile(si, carry):
        rows = pl.ds(pl.multiple_of((first + si) * MOE_SUB, MOE_SUB), MOE_SUB)
        hb = hs_ref[rows, :]
        hg = dot(hb, wg_ref[...])
        hu = dot(hb, wu_ref[...])
        he = (hg * jax.nn.sigmoid(hg) * hu).astype(BF16)
        ce = jnp.sum(jnp.where(lane_s == e + ROUTE_EXPERT0, cs_ref[rows, :], 0.0), axis=1, keepdims=True)
        ys_ref[rows, :] = (ys_ref[rows, :].astype(F32) + ce * dot(he, wd_ref[...])).astype(BF16)
        return carry

    lax.fori_loop(0, meta_ref[N_GROUPS + grp], sub_tile, 0)

    @pl.when(e == pl.num_programs(1) - 1)
    def _():
        c_i = lax.broadcasted_iota(jnp.int32, (MOE_CHUNK, MOE_ROWS), 1).astype(F32)
        for c0 in range(0, tm, MOE_CHUNK):
            unperm = jnp.where(pos_ref[c0:c0 + MOE_CHUNK, 0:1] == c_i, 1.0, 0.0).astype(BF16)
            o_ref[c0:c0 + MOE_CHUNK, :] += dot(unperm, ys_ref[...])


def moe_block(x, g, wr_hi, wr_lo, br, w_gate, w_up, w_down):
    T, D = x.shape
    E, _, Fd = w_gate.shape
    tm = MOE_TM
    row = lambda i, e: (i, 0)
    fixed = lambda i, e: (0, 0)
    return pl.pallas_call(
        _moe_kernel,
        grid=(T // tm, E),
        in_specs=[
            pl.BlockSpec((tm, D), row, pipeline_mode=pl.Buffered(1)),
            pl.BlockSpec((1, D), fixed),
            pl.BlockSpec((D, LANES), fixed),
            pl.BlockSpec((D, LANES), fixed),
            pl.BlockSpec((1, LANES), fixed),
            pl.BlockSpec((None, D, Fd), lambda i, e: (e, 0, 0)),
            pl.BlockSpec((None, D, Fd), lambda i, e: (e, 0, 0)),
            pl.BlockSpec((None, Fd, D), lambda i, e: (e, 0, 0)),
        ],
        out_specs=pl.BlockSpec((tm, D), row, pipeline_mode=pl.Buffered(1)),
        out_shape=jax.ShapeDtypeStruct((T, D), F32),
        scratch_shapes=[pltpu.VMEM((MOE_ROWS, D), BF16), pltpu.VMEM((MOE_ROWS, D), BF16),
                        pltpu.VMEM((MOE_ROWS, LANES), F32), pltpu.VMEM((tm, LANES), F32),
                        pltpu.SMEM((2 * N_GROUPS,), jnp.int32)],
        compiler_params=_cparams(("parallel", "arbitrary")),
        name="moe_block",
    )(x, g, wr_hi, wr_lo, br, w_gate, w_up, w_down)


def _prep_route(w_route_group, b_route_group, w_route_expert, b_route_expert):
    D = w_route_group.shape[0]
    pad = LANES - N_GROUPS - N_EXPERTS
    wr = jnp.concatenate([w_route_group, w_route_expert, jnp.zeros((D, pad), F32)], axis=1)
    br = jnp.concatenate([b_route_group, b_route_expert, jnp.zeros((pad,), F32)])[None]
    hi = wr.astype(BF16)
    lo = (wr - hi.astype(F32)).astype(BF16)
    return hi, lo, br


def _prep_lora(w_decay_up, w_aicl_up, w_gate_lora_up):
    zw = jnp.zeros((LANES - DECAY_LORA, RWKV_WIDTH), w_decay_up.dtype)
    wd = jnp.concatenate([w_decay_up, zw], axis=0)
    wa = jnp.concatenate([jnp.zeros((DECAY_LORA, RWKV_WIDTH), w_aicl_up.dtype), w_aicl_up], axis=0)
    wg = jnp.concatenate([w_gate_lora_up,
                          jnp.zeros((2 * LANES - GATE_LORA, RWKV_WIDTH), w_gate_lora_up.dtype)], axis=0)
    return wd.astype(BF16), wa.astype(BF16), wg.astype(BF16)


DSA_COLS = 4176
IN_PROJ_TILES = 2
IN_PROJ_ROWS = 512
RWKV_SPLITS = (RWKV_WIDTH, DECAY_LORA, RWKV_WIDTH, RWKV_WIDTH, AICL_LORA, GATE_LORA)


def _rwkv_cols(t):
    cuts = np.cumsum(RWKV_SPLITS)[:-1].tolist()
    r, dw, k, v, da, dg = jnp.split(t, cuts, axis=-1)
    pad = jnp.zeros(t.shape[:-1] + (RWKV_PAD - sum(RWKV_SPLITS),), t.dtype)
    return jnp.concatenate([r, k, v, dw, da, dg, pad], axis=-1)


def _prep_in_weights(w_in):
    pad = jnp.zeros((w_in.shape[0], DSA_PAD - DSA_COLS), w_in.dtype)
    w_dsa = jnp.concatenate([w_in[:, :DSA_COLS], pad], axis=1)
    w_rwkv = _rwkv_cols(w_in[:, DSA_COLS:])

    def tiles(w):
        d, n = w.shape
        return w.astype(BF16).reshape(d, IN_PROJ_TILES, n // IN_PROJ_TILES).transpose(1, 0, 2)

    return tiles(w_dsa), tiles(w_rwkv)


def kernel(x, g_mix, w_in, rwkv_shift_mix, q_gain, k_gain, w0, w_decay_up, a0, w_aicl_up, w_gate_lora_up, k_k, k_a, r_k, ln_x_w, ln_x_b, w_out, g_ffn, w_route_group, b_route_group, w_route_expert, b_route_expert, w_e_gate, w_e_up, w_e_down):
    B, S, D = x.shape
    T = B * S
    depth = g_mix.shape[0]
    xf = x.reshape(T, D)
    vec = lambda t: t.reshape(1, -1)
    for l in range(depth):
        w_dsa, w_rwkv = _prep_in_weights(w_in[l])
        p_dsa = norm_matmul(xf, vec(g_mix[l]), w_dsa, IN_PROJ_ROWS)
        p_rwkv = norm_matmul(xf, vec(g_mix[l]), w_rwkv, IN_PROJ_ROWS)
        q, k, v, qi, ki, w = dsa_prep(p_dsa, vec(q_gain[l]), vec(k_gain[l]), B, S, DSA_TILE)
        y_dsa = dsa_attention(q, k, v, qi, ki, w, B, S, DSA_TILE, DSA_TILE)
        wd, wa, wg = _prep_lora(w_decay_up[l], w_aicl_up[l], w_gate_lora_up[l])
        y_rwkv = rwkv_mixer(p_rwkv, _rwkv_cols(vec(rwkv_shift_mix[l])), vec(w0[l]), vec(a0[l]),
                            vec(k_k[l]), vec(k_a[l]), vec(r_k[l]), vec(ln_x_w[l]), vec(ln_x_b[l]),
                            wd, wa, wg, B, S, CHUNK)
        xf = out_proj(xf, y_dsa, y_rwkv, w_out[l].astype(BF16), 512)
        wr_hi, wr_lo, br = _prep_route(w_route_group[l], b_route_group[l],
                                       w_route_expert[l], b_route_expert[l])
        xf = moe_block(xf, vec(g_ffn[l]), wr_hi, wr_lo, br, w_e_gate[l].astype(BF16),
                       w_e_up[l].astype(BF16), w_e_down[l].astype(BF16))
    return xf.reshape(B, S, D)
```

```python
import functools

import jax
import jax.numpy as jnp
import numpy as np
from jax import lax
from jax.experimental import pallas as pl
from jax.experimental.pallas import tpu as pltpu

F32 = jnp.float32
BF16 = jnp.bfloat16

NORM_EPS = 1e-6
CHUNK = 64
DSA_HEADS = 8
DSA_HEAD_DIM = 128
DSA_WIDTH = DSA_HEADS * DSA_HEAD_DIM
IDX_HEADS = 16
IDX_DIM = 64
TOPK_MAX = 256
ROPE_THETA = 10000.0
RWKV_HEADS = 16
RWKV_HEAD_DIM = 64
RWKV_WIDTH = RWKV_HEADS * RWKV_HEAD_DIM
DECAY_LORA = 64
AICL_LORA = 64
GATE_LORA = 160
GN_EPS = 64e-5
N_GROUPS = 4
EXPERTS_PER_GROUP = 4
N_EXPERTS = 16
D_EXPERT = 512

LANES = 128
SUBLANES = 8
VMEM_LIMIT = 56 * 1024 * 1024

DSA_PAD = 4352
RWKV_PAD = 3584
DSA_SMALL = 4096
RWKV_SMALL = 3072

NEG_INF = float("-inf")
INT_MIN = -(2 ** 31)
LOG2E = 1.4426950408889634


def _cparams(sem):
    return pltpu.CompilerParams(dimension_semantics=sem, vmem_limit_bytes=VMEM_LIMIT)


def _norm_matmul_kernel(x_ref, g_ref, w_ref, o_ref, h_ref):
    @pl.when(pl.program_id(1) == 0)
    def _():
        x = x_ref[...]
        ms = jnp.mean(x * x, axis=-1, keepdims=True)
        h_ref[...] = (x * lax.rsqrt(ms + NORM_EPS) * g_ref[...]).astype(BF16)

    o_ref[...] = jnp.dot(h_ref[...], w_ref[...], preferred_element_type=F32)


def norm_matmul(x, g, w, tm):
    T, D = x.shape
    nt, _, tn = w.shape
    N = nt * tn
    return pl.pallas_call(
        _norm_matmul_kernel,
        grid=(T // tm, nt),
        in_specs=[
            pl.BlockSpec((tm, D), lambda i, j: (i, 0)),
            pl.BlockSpec((1, D), lambda i, j: (0, 0)),
            pl.BlockSpec((None, D, tn), lambda i, j: (j, 0, 0)),
        ],
        out_specs=pl.BlockSpec((tm, tn), lambda i, j: (i, j)),
        out_shape=jax.ShapeDtypeStruct((T, N), F32),
        scratch_shapes=[pltpu.VMEM((tm, D), BF16)],
        compiler_params=_cparams(("parallel", "arbitrary")),
        name="norm_matmul",
    )(x, g, w)


def _rope_tables(S, d):
    half = d // 2
    inv = ROPE_THETA ** (-jnp.arange(half, dtype=F32) * (2.0 / d))
    ang = jnp.arange(S, dtype=F32)[:, None] * inv[None, :]
    cos, sin = jnp.cos(ang), jnp.sin(ang)
    return jnp.concatenate([cos, cos], axis=-1), jnp.concatenate([-sin, sin], axis=-1)


def _dsa_prep_kernel(p_ref, qg_ref, kg_ref, c128_ref, s128_ref, c64_ref, s64_ref,
                     qt_ref, k_ref, vt_ref, qit_ref, ki_ref, wt_ref):
    c128, s128 = c128_ref[...], s128_ref[...]
    c64, s64 = c64_ref[...], s64_ref[...]
    tm = c128.shape[0]
    first_half = (lax.broadcasted_iota(jnp.int32, (tm, LANES), 1) % IDX_DIM) < IDX_DIM // 2

    def rope128(y):
        return y * c128 + pltpu.roll(y, DSA_HEAD_DIM // 2, 1) * s128

    def rope64(y):
        partner = jnp.where(first_half, pltpu.roll(y, LANES - IDX_DIM // 2, 1),
                            pltpu.roll(y, IDX_DIM // 2, 1))
        return y * c64 + partner * s64

    ones = jnp.ones((LANES, LANES), BF16)
    e_i = lax.broadcasted_iota(jnp.int32, (LANES, LANES), 0)
    e_j = lax.broadcasted_iota(jnp.int32, (LANES, LANES), 1)
    eye = jnp.where(e_i == e_j, 1.0, 0.0).astype(BF16)

    def normed(x, gain):
        sq = x * x
        hi = sq.astype(BF16)
        lo = (sq - hi.astype(F32)).astype(BF16)
        ms = (jnp.dot(hi, ones, preferred_element_type=F32)
              + jnp.dot(lo, ones, preferred_element_type=F32)) * (1.0 / DSA_HEAD_DIM)
        return x * lax.rsqrt(ms + NORM_EPS) * gain

    def transposed(y):
        return _nt_dot(eye, y.astype(BF16)).astype(BF16)

    scale = DSA_HEAD_DIM ** -0.5 * LOG2E
    for h in range(DSA_HEADS):
        lo = h * DSA_HEAD_DIM
        q = p_ref[:, lo:lo + DSA_HEAD_DIM]
        qt_ref[lo:lo + DSA_HEAD_DIM, :] = transposed(rope128(normed(q, qg_ref[...])) * scale)
        k = p_ref[:, DSA_WIDTH + lo:DSA_WIDTH + lo + DSA_HEAD_DIM]
        k_ref[:, lo:lo + DSA_HEAD_DIM] = rope128(normed(k, kg_ref[...])).astype(BF16)
        v = p_ref[:, 2 * DSA_WIDTH + lo:2 * DSA_WIDTH + lo + DSA_HEAD_DIM]
        vt_ref[lo:lo + DSA_HEAD_DIM, :] = transposed(v)
    for j in range(IDX_HEADS * IDX_DIM // LANES):
        lo = 3 * DSA_WIDTH + j * LANES
        qit_ref[j * LANES:(j + 1) * LANES, :] = transposed(rope64(p_ref[:, lo:lo + LANES]))
    small = p_ref[:, DSA_SMALL:DSA_SMALL + LANES]
    lane = lax.broadcasted_iota(jnp.int32, (tm, LANES), 1)
    ki = rope64(small)
    ki_ref[:, :LANES] = jnp.where(lane < IDX_DIM, ki, 0.0).astype(BF16)
    ki_ref[:, LANES:] = jnp.where(lane >= IDX_DIM, pltpu.roll(ki, IDX_DIM, 1), 0.0).astype(BF16)
    wt_ref[...] = (small * (IDX_HEADS ** -0.5 * IDX_DIM ** -0.5)).T


def dsa_prep(p_dsa, q_gain, k_gain, B, S, tm):
    T = B * S
    c128, s128 = _rope_tables(S, DSA_HEAD_DIM)
    c64, s64 = _rope_tables(S, IDX_DIM)
    c64 = jnp.concatenate([c64, c64], axis=-1)
    s64 = jnp.concatenate([s64, s64], axis=-1)
    nsb = S // tm
    row = lambda i: (i, 0)
    pos = lambda i: (i % nsb, 0)
    fixed = lambda i: (0, 0)
    tile_t = lambda i: (i // nsb, i % nsb, 0, 0)
    wide_t = pl.BlockSpec((None, None, DSA_WIDTH, tm), tile_t)
    return pl.pallas_call(
        _dsa_prep_kernel,
        grid=(T // tm,),
        in_specs=[
            pl.BlockSpec((tm, DSA_PAD), row),
            pl.BlockSpec((1, DSA_HEAD_DIM), fixed),
            pl.BlockSpec((1, DSA_HEAD_DIM), fixed),
            pl.BlockSpec((tm, LANES), pos),
            pl.BlockSpec((tm, LANES), pos),
            pl.BlockSpec((tm, LANES), pos),
            pl.BlockSpec((tm, LANES), pos),
        ],
        out_specs=[
            wide_t,
            pl.BlockSpec((tm, DSA_WIDTH), row),
            wide_t,
            wide_t,
            pl.BlockSpec((tm, 2 * LANES), row),
            pl.BlockSpec((None, None, LANES, tm), tile_t),
        ],
        out_shape=[
            jax.ShapeDtypeStruct((B, nsb, DSA_WIDTH, tm), BF16),
            jax.ShapeDtypeStruct((T, DSA_WIDTH), BF16),
            jax.ShapeDtypeStruct((B, nsb, DSA_WIDTH, tm), BF16),
            jax.ShapeDtypeStruct((B, nsb, IDX_HEADS * IDX_DIM, tm), BF16),
            jax.ShapeDtypeStruct((T, 2 * LANES), BF16),
            jax.ShapeDtypeStruct((B, nsb, LANES, tm), F32),
        ],
        compiler_params=_cparams(("parallel",)),
        name="dsa_prep",
    )(p_dsa, q_gain, k_gain, c128, s128, c64, s64)


SCORE_ROWS = 128
DSA_TILE = 256
SEARCH_BLIND = 16
SEARCH_CAP = 40
PEEL_FROM = 16
INF = float("inf")
SHIFT_BOUND_MAX = 50.0
ONES_ROWS = 16


def _nt_dot(a, b):
    return lax.dot_general(a, b, (((1,), (1,)), ((), ())), preferred_element_type=F32)


def _dsa_kernel(qt_ref, k_ref, vt_ref, qit_ref, ki_ref, wt_ref, y_ref,
                sc_ref, bias_ref, thr_ref, m_ref, l_ref, acc_ref, ksq_ref,
                s0_ref, s1_ref, p0_ref, p1_ref, *, tq, tk, top_k):
    i = pl.program_id(1)
    nkb = ((i + 1) * tq + tk - 1) // tk
    qpos = i * tq + lax.broadcasted_iota(jnp.int32, (1, tq), 1)
    limit = (qpos // CHUNK + 1) * CHUNK
    kf = float(top_k)

    def blk(kb):
        return pl.ds(pl.multiple_of(kb * tk, tk), tk)

    def key_pos(kb, rows, r0=0):
        return kb * tk + r0 + lax.broadcasted_iota(jnp.int32, (rows, tq), 0)

    def score_block(kb, carry):
        mn, mx = carry
        for r0 in range(0, tk, SCORE_ROWS):
            kblk = ki_ref[pl.ds(pl.multiple_of(kb * tk, tk) + r0, SCORE_ROWS), :]
            acc = jnp.zeros((SCORE_ROWS, tq), F32)
            for j in range(IDX_HEADS // 2):
                qblk = qit_ref[j * LANES:(j + 1) * LANES, :]
                s_even = jnp.dot(kblk[:, :LANES], qblk, preferred_element_type=F32)
                s_odd = jnp.dot(kblk[:, LANES:], qblk, preferred_element_type=F32)
                acc = acc + jnp.maximum(s_even, 0.0) * wt_ref[IDX_DIM + 2 * j:IDX_DIM + 2 * j + 1, :]
                acc = acc + jnp.maximum(s_odd, 0.0) * wt_ref[IDX_DIM + 2 * j + 1:IDX_DIM + 2 * j + 2, :]
            adm = key_pos(kb, SCORE_ROWS, r0) < limit
            sc_ref[pl.ds(pl.multiple_of(kb * tk, tk) + r0, SCORE_ROWS), :] = jnp.where(adm, acc, NEG_INF)
            mn = jnp.minimum(mn, jnp.min(jnp.where(adm, acc, INF), axis=0, keepdims=True))
            mx = jnp.maximum(mx, jnp.max(jnp.where(adm, acc, NEG_INF), axis=0, keepdims=True))
        return mn, mx

    row_min, row_max = lax.fori_loop(
        0, nkb, score_block, (jnp.full((1, tq), INF, F32), jnp.full((1, tq), NEG_INF, F32)))

    def fold8(x, op):
        return op(x.reshape(x.shape[0] // SUBLANES, SUBLANES, tq), axis=0)

    def count_where(pred):
        def body(kb, acc):
            return acc + fold8(jnp.where(pred(sc_ref[blk(kb), :]), 1.0, 0.0), jnp.sum)
        part = lax.fori_loop(0, nkb, body, jnp.zeros((SUBLANES, tq), F32))
        return jnp.sum(part, axis=0, keepdims=True)

    def count_ge(x):
        return count_where(lambda s: s >= x)

    def max_below(x):
        def body(kb, acc):
            s = sc_ref[blk(kb), :]
            return jnp.maximum(acc, fold8(jnp.where(s < x, s, NEG_INF), jnp.max))
        part = lax.fori_loop(0, nkb, body, jnp.full((SUBLANES, tq), NEG_INF, F32))
        return jnp.max(part, axis=0, keepdims=True)

    take_all = limit <= top_k

    def search_cond(st):
        it, done = st[0], st[1]
        return jnp.logical_and(it < SEARCH_CAP, jnp.min(done) < 0.5)

    def search_step(st, may_peel=True):
        it, done, thr, lo, hi = st
        mid = 0.5 * lo + 0.5 * hi
        peel = jnp.logical_and(it >= PEEL_FROM, it % 4 == 3) if may_peel else False
        if may_peel:
            mid = lax.cond(peel, lambda: max_below(hi), lambda: mid)
        c = count_ge(mid)
        active = done < 0.5
        hit = jnp.logical_or(c == kf, jnp.logical_and(peel, c >= kf))
        fin = jnp.logical_and(active, hit)
        up = jnp.logical_and(active, jnp.logical_and(jnp.logical_not(hit), c > kf))
        dn = jnp.logical_and(active, c < kf)
        return (it + 1, jnp.where(fin, 1.0, done), jnp.where(fin, mid, thr),
                jnp.where(up, mid, lo), jnp.where(dn, mid, hi))

    done0 = jnp.where(take_all, 1.0, 0.0)
    st = (jnp.int32(0), done0, jnp.full((1, tq), NEG_INF, F32), row_min,
          row_max + (jnp.abs(row_max) * 2.0 ** -10 + 1e-30))
    st = lax.fori_loop(0, SEARCH_BLIND, lambda _, s: search_step(s, may_peel=False), st)
    st = lax.while_loop(search_cond, search_step, st)
    done, thr = st[1], st[2]
    thr_ref[...] = thr

    @pl.when(jnp.min(done) < 0.5)
    def _():
        def from_key(key):
            return pltpu.bitcast(jnp.where(key < 0, key ^ jnp.int32(0x7FFFFFFF), key), F32)

        def bit_step(b, key):
            cand = key ^ jnp.left_shift(jnp.int32(1), 31 - b)
            return jnp.where(count_ge(from_key(cand)) >= kf, cand, key)

        key = lax.fori_loop(0, 32, bit_step, jnp.full((1, tq), INT_MIN, jnp.int32))
        thr_ref[...] = jnp.where(done < 0.5, from_key(key), thr)

    thr = thr_ref[...]
    tied = jnp.logical_and(count_ge(thr) > kf, jnp.logical_not(take_all))
    any_tied = jnp.max(jnp.where(tied, 1.0, 0.0)) > 0.5

    @pl.when(jnp.logical_not(any_tied))
    def _():
        def body(kb, carry):
            sel = jnp.logical_and(sc_ref[blk(kb), :] >= thr, key_pos(kb, tk) < limit)
            bias_ref[blk(kb), :] = jnp.where(sel, 0.0, NEG_INF)
            return carry
        lax.fori_loop(0, nkb, body, 0)

    @pl.when(any_tied)
    def _():
        need = kf - count_where(lambda s: s > thr)
        r_i = lax.broadcasted_iota(jnp.int32, (tk, tk), 0)
        c_i = lax.broadcasted_iota(jnp.int32, (tk, tk), 1)
        lower = jnp.where(c_i <= r_i, 1.0, 0.0).astype(BF16)

        def body(kb, seen):
            s = sc_ref[blk(kb), :]
            eq = s == thr
            prefix = seen + jnp.dot(lower, jnp.where(eq, 1.0, 0.0).astype(BF16),
                                    preferred_element_type=F32)
            sel = jnp.logical_or(s > thr, jnp.logical_and(eq, prefix <= need))
            sel = jnp.logical_or(sel, take_all)
            sel = jnp.logical_and(sel, key_pos(kb, tk) < limit)
            bias_ref[blk(kb), :] = jnp.where(sel, 0.0, NEG_INF)
            return prefix[tk - 1:tk, :]
        lax.fori_loop(0, nkb, body, jnp.zeros((1, tq), F32))

    heads = [slice(h * DSA_HEAD_DIM, (h + 1) * DSA_HEAD_DIM) for h in range(DSA_HEADS)]

    @pl.when(i == 0)
    def _():
        def body(kb, carry):
            out = []
            for h, hs in enumerate(heads):
                kh = k_ref[blk(kb), hs].astype(F32)
                sq = jnp.max(jnp.sum(kh * kh, axis=1, keepdims=True), axis=0, keepdims=True)
                out.append(jnp.maximum(carry[h], sq))
            return tuple(out)
        ksq = lax.fori_loop(0, k_ref.shape[0] // tk, body,
                            tuple(jnp.zeros((1, 1), F32) for _ in heads))
        for h in range(DSA_HEADS):
            ksq_ref[h] = jnp.broadcast_to(ksq[h], (1, tq))

    worst = jnp.zeros((1, tq), F32)
    for h, hs in enumerate(heads):
        qh = qt_ref[hs, :].astype(F32)
        bound = jnp.sqrt(jnp.sum(qh * qh, axis=0, keepdims=True) * ksq_ref[h])
        m_ref[h] = bound
        worst = jnp.maximum(worst, bound)
    shift_ok = jnp.max(worst) <= SHIFT_BOUND_MAX
    acc_ref[...] = jnp.zeros(acc_ref.shape, F32)

    @pl.when(shift_ok)
    def _():
        ones = jnp.ones((ONES_ROWS, tk), BF16)
        n_blocks = k_ref.shape[0] // tk
        bias_ref[pl.ds(n_blocks * tk, tk), :] = jnp.full((tk, tq), NEG_INF, F32)
        s1_ref[...] = jnp.zeros(s1_ref.shape, F32)
        p0_ref[...] = jnp.zeros(p0_ref.shape, BF16)

        def step(t, s_w, s_r, p_w, p_r):
            ta = jnp.minimum(t, nkb - 1)
            for h, hs in enumerate(heads):
                s_w[h] = jnp.dot(k_ref[blk(ta), hs], qt_ref[hs, :], preferred_element_type=F32)
            tb = t - 1
            valid = jnp.logical_and(tb >= 0, tb < nkb)
            bias = bias_ref[blk(jnp.where(valid, tb, n_blocks)), :]
            for h in range(DSA_HEADS):
                p_w[h] = jnp.exp2(s_r[h] + bias - m_ref[h]).astype(BF16)
            tc = jnp.maximum(t - 2, 0)
            for h, hs in enumerate(heads):
                acc_ref[h] += jnp.dot(jnp.concatenate([vt_ref[tc, hs, :], ones], axis=0), p_r[h],
                                      preferred_element_type=F32)

        def one_step(t, carry):
            @pl.when(t % 2 == 0)
            def _():
                step(t, s0_ref, s1_ref, p1_ref, p0_ref)

            @pl.when(t % 2 == 1)
            def _():
                step(t, s1_ref, s0_ref, p0_ref, p1_ref)
            return carry

        lax.fori_loop(0, nkb + 2, one_step, 0)
        for h, hs in enumerate(heads):
            acc = acc_ref[h]
            y = acc[:DSA_HEAD_DIM] / acc[DSA_HEAD_DIM:DSA_HEAD_DIM + 1]
            y_ref[:, hs] = y.T.astype(y_ref.dtype)

    @pl.when(jnp.logical_not(shift_ok))
    def _():
        m_ref[...] = jnp.full(m_ref.shape, NEG_INF, F32)
        l_ref[...] = jnp.zeros(l_ref.shape, F32)

        def attend(kb, carry):
            bias = bias_ref[blk(kb), :]
            for h, hs in enumerate(heads):
                s = jnp.dot(k_ref[blk(kb), hs], qt_ref[hs, :], preferred_element_type=F32) + bias
                m = m_ref[h]
                m_new = jnp.maximum(m, jnp.max(s, axis=0, keepdims=True))
                m_use = jnp.where(m_new == NEG_INF, 0.0, m_new)
                p = jnp.exp2(s - m_use)
                alpha = jnp.exp2(m - m_use)
                m_ref[h] = m_new
                l_ref[h] = alpha * l_ref[h] + jnp.sum(p, axis=0, keepdims=True)
                acc_ref[h, :DSA_HEAD_DIM, :] = (
                    alpha * acc_ref[h, :DSA_HEAD_DIM, :]
                    + jnp.dot(vt_ref[kb, hs, :], p.astype(BF16), preferred_element_type=F32))
            return carry

        lax.fori_loop(0, nkb, attend, 0)
        for h, hs in enumerate(heads):
            y_ref[:, hs] = (acc_ref[h, :DSA_HEAD_DIM, :] / l_ref[h]).T.astype(y_ref.dtype)


def dsa_attention(qt, k, vt, qit, ki, wt, B, S, tq, tk):
    assert tq == tk and qt.shape[-1] == tq
    T = B * S
    nq = S // tq
    top_k = min(TOPK_MAX, S // 4)
    qtile = lambda b, i: (b, i, 0, 0)
    seq = lambda b, i: (b, 0)
    return pl.pallas_call(
        functools.partial(_dsa_kernel, tq=tq, tk=tk, top_k=top_k),
        grid=(B, nq),
        in_specs=[
            pl.BlockSpec((None, None, DSA_WIDTH, tq), qtile),
            pl.BlockSpec((S, DSA_WIDTH), seq),
            pl.BlockSpec((None, nq, DSA_WIDTH, tk), lambda b, i: (b, 0, 0, 0)),
            pl.BlockSpec((None, None, IDX_HEADS * IDX_DIM, tq), qtile),
            pl.BlockSpec((S, 2 * LANES), seq),
            pl.BlockSpec((None, None, LANES, tq), qtile),
        ],
        out_specs=pl.BlockSpec((tq, DSA_WIDTH), lambda b, i: (b * nq + i, 0)),
        out_shape=jax.ShapeDtypeStruct((T, DSA_WIDTH), BF16),
        scratch_shapes=[pltpu.VMEM((S, tq), F32), pltpu.VMEM((S + tk, tq), F32),
                        pltpu.VMEM((1, tq), F32),
                        pltpu.VMEM((DSA_HEADS, 1, tq), F32), pltpu.VMEM((DSA_HEADS, 1, tq), F32),
                        pltpu.VMEM((DSA_HEADS, DSA_HEAD_DIM + ONES_ROWS, tq), F32),
                        pltpu.VMEM((DSA_HEADS, 1, tq), F32),
                        pltpu.VMEM((DSA_HEADS, tk, tq), F32), pltpu.VMEM((DSA_HEADS, tk, tq), F32),
                        pltpu.VMEM((DSA_HEADS, tk, tq), BF16), pltpu.VMEM((DSA_HEADS, tk, tq), BF16)],
        compiler_params=_cparams(("parallel", "arbitrary")),
        name="dsa_attention",
    )(qt, k, vt, qit, ki, wt)


RW_PAIRS = RWKV_HEADS // 2
RW_GROUPS = RWKV_HEADS // 4


def _split3(x):
    hi = x.astype(BF16)
    r1 = x - hi.astype(F32)
    mid = r1.astype(BF16)
    lo = (r1 - mid.astype(F32)).astype(BF16)
    return hi, mid, lo


def _rwkv_kernel(p_ref, mu_ref, w0_ref, a0_ref, kk_ref, ka_ref, rk_ref, lnw_ref, lnb_ref,
                 wd_ref, wa_ref, wg_ref, y_ref, state_ref, prev_ref, *, C):
    c = pl.program_id(1)

    @pl.when(c == 0)
    def _():
        state_ref[...] = jnp.zeros_like(state_ref)
        prev_ref[...] = jnp.zeros_like(prev_ref)

    W = RWKV_WIDTH
    dot = functools.partial(jnp.dot, preferred_element_type=F32)

    p = p_ref[...]
    first_row = lax.broadcasted_iota(jnp.int32, p.shape, 0) == 0
    prev = jnp.where(first_row, prev_ref[...], pltpu.roll(p, 1, 0))
    prev_ref[...] = p[C - 1:C, :]
    xs = p + (prev - p) * mu_ref[...]

    small = xs[:, RWKV_SMALL:RWKV_SMALL + LANES]
    gate_in = xs[:, RWKV_SMALL + LANES:RWKV_SMALL + 3 * LANES]
    z = w0_ref[...] + dot(jnp.tanh(small).astype(BF16), wd_ref[...])
    softplus = jnp.maximum(-z, 0.0) + jnp.log(1.0 + jnp.exp(-jnp.abs(z)))
    logdec = -jnp.exp(-softplus - 0.5)
    rate = jax.nn.sigmoid(a0_ref[...] + dot(small.astype(BF16), wa_ref[...]))
    gate = dot(jax.nn.sigmoid(gate_in).astype(BF16), wg_ref[...])

    ti = lax.broadcasted_iota(jnp.int32, (C, C), 0)
    tj = lax.broadcasted_iota(jnp.int32, (C, C), 1)
    lower = jnp.where(tj <= ti, 1.0, 0.0).astype(BF16)
    hi, mid, lo = _split3(logdec)
    cum = dot(lower, hi) + dot(lower, mid) + dot(lower, lo)

    gi = lax.broadcasted_iota(jnp.int32, (LANES, LANES), 0) // RWKV_HEAD_DIM
    gj = lax.broadcasted_iota(jnp.int32, (LANES, LANES), 1) // RWKV_HEAD_DIM
    same_head = jnp.where(gi == gj, 1.0, 0.0).astype(BF16)

    def head_sum(parts):
        x = jnp.concatenate(parts, axis=0)
        xh = x.astype(BF16)
        xl = (x - xh.astype(F32)).astype(BF16)
        s = dot(xh, same_head) + dot(xl, same_head)
        return [s[n * C:(n + 1) * C] for n in range(len(parts))]

    lane = lax.broadcasted_iota(jnp.int32, (C, LANES), 1)
    even = lane < RWKV_HEAD_DIM
    zeros = jnp.zeros((C, LANES), F32)

    pairs = []
    for pi in range(RW_PAIRS):
        sl = slice(pi * LANES, (pi + 1) * LANES)
        r = xs[:, sl]
        k = xs[:, W + pi * LANES:W + (pi + 1) * LANES]
        v = xs[:, 2 * W + pi * LANES:2 * W + (pi + 1) * LANES]
        a = rate[:, sl]
        kk = k * kk_ref[:, sl]
        k2 = k * (1.0 + (a - 1.0) * ka_ref[:, sl])
        pairs.append(dict(r=r, v=v, a=a, kk=kk, k2=k2, sl=sl))
    sums = head_sum([d["kk"] * d["kk"] for d in pairs] + [d["r"] * d["k2"] * rk_ref[:, d["sl"]] for d in pairs])
    for pi, d in enumerate(pairs):
        sl = d["sl"]
        kkn = d["kk"] / jnp.maximum(jnp.sqrt(sums[pi]), 1e-12)
        d["bonus"] = sums[RW_PAIRS + pi] * d["v"]
        b = kkn * d["a"]
        cm = cum[:, sl]
        last = cm[C - 1:C, :]
        e_neg = jnp.exp(-cm)
        e_rem = jnp.exp(last - cm)
        d["At"] = -kkn * jnp.exp(cm - logdec[:, sl])
        d["Rt"] = d["r"] * jnp.exp(cm)
        d["Bt"] = b * e_neg
        d["Kt"] = d["k2"] * e_neg
        d["Bh"] = b * e_rem
        d["Kh"] = d["k2"] * e_rem
        d["wlast"] = jnp.exp(last)

    G = 4 * C
    ri = lax.broadcasted_iota(jnp.int32, (G, G), 0)
    ci = lax.broadcasted_iota(jnp.int32, (G, G), 1)
    same_block = (ri // C) == (ci // C)
    strict = jnp.logical_and(same_block, ci < ri)
    incl = jnp.logical_and(same_block, ci <= ri)
    pair_block = jnp.where(gi == gj, 1.0, 0.0)

    def masked4(x0, x1):
        return jnp.concatenate([
            jnp.concatenate([jnp.where(even, x0, 0.0), zeros], axis=1),
            jnp.concatenate([jnp.where(even, 0.0, x0), zeros], axis=1),
            jnp.concatenate([zeros, jnp.where(even, x1, 0.0)], axis=1),
            jnp.concatenate([zeros, jnp.where(even, 0.0, x1)], axis=1)], axis=0).astype(BF16)

    def plain4(x0, x1):
        return jnp.concatenate([
            jnp.concatenate([x0, zeros], axis=1), jnp.concatenate([x0, zeros], axis=1),
            jnp.concatenate([zeros, x1], axis=1), jnp.concatenate([zeros, x1], axis=1)],
            axis=0).astype(BF16)

    groups = []
    for g in range(RW_GROUPS):
        d0, d1 = pairs[2 * g], pairs[2 * g + 1]
        lhs_a = masked4(d0["At"], d1["At"])
        lhs_r = masked4(d0["Rt"], d1["Rt"])
        rhs_b = plain4(d0["Bt"], d1["Bt"])
        rhs_k = plain4(d0["Kt"], d1["Kt"])
        groups.append(dict(
            d=(d0, d1),
            m=jnp.where(strict, _nt_dot(lhs_a, rhs_b), 0.0).astype(BF16),
            m_ak=jnp.where(strict, _nt_dot(lhs_a, rhs_k), 0.0).astype(BF16),
            r_b=jnp.where(incl, _nt_dot(lhs_r, rhs_b), 0.0).astype(BF16),
            r_k=jnp.where(incl, _nt_dot(lhs_r, rhs_k), 0.0).astype(BF16),
            v4=jnp.concatenate([d0["v"], d0["v"], d1["v"], d1["v"]], axis=0).astype(BF16)))
    for g, grp in enumerate(groups):
        a0s, y0s = [], []
        for n, d in enumerate(grp["d"]):
            st = state_ref[2 * g + n].astype(BF16)
            both = _nt_dot(jnp.concatenate([d["At"], d["Rt"]], axis=0).astype(BF16), st)
            a0s.append(both[:C])
            y0s.append(both[C:])
        grp["y0"] = y0s
        grp["x"] = (jnp.concatenate([a0s[0], a0s[0], a0s[1], a0s[1]], axis=0)
                    + dot(grp["m_ak"], grp["v4"]))
    steps = max(1, int(np.ceil(np.log2(C))))
    for s in range(steps):
        for grp in groups:
            grp["x"] = grp["x"] + dot(grp["m"], grp["x"].astype(BF16))
        if s + 1 < steps:
            for grp in groups:
                grp["m"] = dot(grp["m"], grp["m"]).astype(BF16)
    ys = []
    for g, grp in enumerate(groups):
        x = grp["x"]
        yc = dot(grp["r_b"], x.astype(BF16)) + dot(grp["r_k"], grp["v4"])
        for n, d in enumerate(grp["d"]):
            lo_, hi_ = 2 * n * C, (2 * n + 1) * C
            y = grp["y0"][n] + jnp.where(even, yc[lo_:hi_], yc[hi_:hi_ + C])
            u = jnp.where(even, x[lo_:hi_], x[hi_:hi_ + C])
            uv = jnp.concatenate([u, d["v"]], axis=0).astype(BF16)
            bk = jnp.concatenate([d["Bh"], d["Kh"]], axis=0).astype(BF16)
            upd = lax.dot_general(uv, bk, (((0,), (0,)), ((), ())), preferred_element_type=F32)
            state_ref[2 * g + n] = state_ref[2 * g + n] * d["wlast"] + upd * pair_block
            ys.append(y)

    means = head_sum(ys)
    devs = [y - m_ * (1.0 / RWKV_HEAD_DIM) for y, m_ in zip(ys, means)]
    varis = head_sum([dv * dv for dv in devs])
    for pi, d in enumerate(pairs):
        sl = d["sl"]
        yn = devs[pi] * lax.rsqrt(varis[pi] * (1.0 / RWKV_HEAD_DIM) + GN_EPS)
        yn = yn * lnw_ref[:, sl] + lnb_ref[:, sl]
        y_ref[:, sl] = ((yn + d["bonus"]) * gate[:, sl]).astype(y_ref.dtype)


def rwkv_mixer(p_rwkv, mu, w0, a0, k_k, k_a, r_k, ln_w, ln_b, wd, wa, wg, B, S, C):
    T = B * S
    nc = S // C
    vec = pl.BlockSpec((1, RWKV_WIDTH), lambda b, c: (0, 0))
    return pl.pallas_call(
        functools.partial(_rwkv_kernel, C=C),
        grid=(B, nc),
        in_specs=[
            pl.BlockSpec((C, RWKV_PAD), lambda b, c: (b * nc + c, 0)),
            pl.BlockSpec((1, RWKV_PAD), lambda b, c: (0, 0)),
            vec, vec, vec, vec, vec, vec, vec,
            pl.BlockSpec((LANES, RWKV_WIDTH), lambda b, c: (0, 0)),
            pl.BlockSpec((LANES, RWKV_WIDTH), lambda b, c: (0, 0)),
            pl.BlockSpec((2 * LANES, RWKV_WIDTH), lambda b, c: (0, 0)),
        ],
        out_specs=pl.BlockSpec((C, RWKV_WIDTH), lambda b, c: (b * nc + c, 0)),
        out_shape=jax.ShapeDtypeStruct((T, RWKV_WIDTH), BF16),
        scratch_shapes=[pltpu.VMEM((RW_PAIRS, LANES, LANES), F32),
                        pltpu.VMEM((1, RWKV_PAD), F32)],
        compiler_params=_cparams(("parallel", "arbitrary")),
        name="rwkv_mixer",
    )(p_rwkv, mu, w0, a0, k_k, k_a, r_k, ln_w, ln_b, wd, wa, wg)


def _out_proj_kernel(x_ref, ya_ref, yb_ref, wa_ref, wb_ref, o_ref):
    o_ref[...] = (x_ref[...]
                  + jnp.dot(ya_ref[...], wa_ref[...], preferred_element_type=F32)
                  + jnp.dot(yb_ref[...], wb_ref[...], preferred_element_type=F32))


def out_proj(x, ya, yb, w_out, tm):
    T, D = x.shape
    half = ya.shape[1]
    row = lambda i: (i, 0)
    return pl.pallas_call(
        _out_proj_kernel,
        grid=(T // tm,),
        in_specs=[
            pl.BlockSpec((tm, D), row),
            pl.BlockSpec((tm, half), row),
            pl.BlockSpec((tm, half), row),
            pl.BlockSpec((half, D), lambda i: (0, 0)),
            pl.BlockSpec((half, D), lambda i: (1, 0)),
        ],
        out_specs=pl.BlockSpec((tm, D), row),
        out_shape=jax.ShapeDtypeStruct((T, D), F32),
        compiler_params=_cparams(("parallel",)),
        name="out_proj",
    )(x, ya, yb, w_out, w_out)


ROUTE_EXPERT0 = N_GROUPS


def _route(logits):
    lane = lax.broadcasted_iota(jnp.int32, logits.shape, 1)
    big = jnp.int32(LANES)
    is_grp = lane < N_GROUPS
    gl = jnp.where(is_grp, logits, NEG_INF)
    gmax = jnp.max(gl, axis=1, keepdims=True)
    p_grp = 1.0 / jnp.sum(jnp.exp(gl - gmax), axis=1, keepdims=True)
    g_sel = jnp.min(jnp.where(gl == gmax, lane, big), axis=1, keepdims=True)
    lo = ROUTE_EXPERT0 + g_sel * EXPERTS_PER_GROUP
    in_grp = jnp.logical_and(lane >= lo, lane < lo + EXPERTS_PER_GROUP)
    el = jnp.where(in_grp, logits, NEG_INF)
    ee = jnp.exp(el - jnp.max(el, axis=1, keepdims=True))
    pe = jnp.where(in_grp, ee / jnp.sum(ee, axis=1, keepdims=True), -1.0)
    p1 = jnp.max(pe, axis=1, keepdims=True)
    i1 = jnp.min(jnp.where(pe == p1, lane, big), axis=1, keepdims=True)
    pe2 = jnp.where(lane == i1, -1.0, pe)
    p2 = jnp.max(pe2, axis=1, keepdims=True)
    i2 = jnp.min(jnp.where(pe2 == p2, lane, big), axis=1, keepdims=True)
    tot = p1 + p2
    comb = jnp.where(lane == i1, p_grp * (p1 / tot), jnp.where(lane == i2, p_grp * (p2 / tot), 0.0))
    return comb, g_sel


MOE_TM = 1024
MOE_SUB = 288
MOE_CHUNK = 256
MOE_PAD = (MOE_TM + N_GROUPS * (MOE_SUB - 1)) // MOE_SUB * MOE_SUB
MOE_UNSORT_K = 512
MOE_ROWS = -(-MOE_PAD // MOE_UNSORT_K) * MOE_UNSORT_K


def _moe_kernel(x_ref, g_ref, wr_hi_ref, wr_lo_ref, br_ref, wg_ref, wu_ref, wd_ref, o_ref,
                hs_ref, ys_ref, cs_ref, pos_ref, meta_ref):
    e = pl.program_id(1)
    tm = x_ref.shape[0]
    dot = functools.partial(jnp.dot, preferred_element_type=F32)

    @pl.when(e == 0)
    def _():
        x = x_ref[...]
        ms = jnp.mean(x * x, axis=-1, keepdims=True)
        h = x * lax.rsqrt(ms + NORM_EPS) * g_ref[...]
        o_ref[...] = x
        hi = h.astype(BF16)
        lo = (h - hi.astype(F32)).astype(BF16)
        logits = dot(hi, wr_hi_ref[...]) + dot(lo, wr_hi_ref[...]) + dot(hi, wr_lo_ref[...])
        comb, g_sel = _route(logits + br_ref[...])

        lane = lax.broadcasted_iota(jnp.int32, (tm, LANES), 1)
        onehot = jnp.where(lane == g_sel, 1.0, 0.0)
        onehot_b = onehot.astype(BF16)
        ranks = []
        for c0 in range(0, tm, MOE_CHUNK):
            t_i = c0 + lax.broadcasted_iota(jnp.int32, (MOE_CHUNK, tm), 0)
            t_j = lax.broadcasted_iota(jnp.int32, (MOE_CHUNK, tm), 1)
            ranks.append(dot(jnp.where(t_j < t_i, 1.0, 0.0).astype(BF16), onehot_b))
        rank = jnp.concatenate(ranks, axis=0)
        counts = jnp.sum(onehot, axis=0, keepdims=True)
        padded = jnp.floor((counts + (MOE_SUB - 0.5)) * (1.0 / MOE_SUB)) * MOE_SUB
        lane1 = lax.broadcasted_iota(jnp.int32, (1, LANES), 1)
        base = jnp.zeros((1, LANES), F32)
        start = jnp.zeros((1, 1), F32)
        for g in range(N_GROUPS):
            pad_g = jnp.sum(jnp.where(lane1 == g, padded, 0.0), axis=1, keepdims=True)
            base = jnp.where(lane1 == g, start, base)
            meta_ref[g] = (start[0, 0] * (1.0 / MOE_SUB) + 0.5).astype(jnp.int32)
            meta_ref[N_GROUPS + g] = (pad_g[0, 0] * (1.0 / MOE_SUB) + 0.5).astype(jnp.int32)
            start = start + pad_g
        pos = jnp.sum(onehot * (rank + base), axis=1, keepdims=True)
        pos_b = jnp.broadcast_to(pos, (tm, LANES))
        pos_ref[...] = pos_b
        pos_row = pos_b.T[0:1, :]
        n_used = (start[0, 0] * (1.0 / MOE_SUB) + 0.5).astype(jnp.int32)
        meta_ref[2 * N_GROUPS] = n_used
        c_hi, c_mid, c_lo = _split3(comb)
        for c in range(MOE_PAD // MOE_SUB):
            @pl.when(c < n_used)
            def _(c0=c * MOE_SUB):
                r_i = (c0 + lax.broadcasted_iota(jnp.int32, (MOE_SUB, tm), 0)).astype(F32)
                perm = jnp.where(r_i == pos_row, 1.0, 0.0).astype(BF16)
                hs_ref[c0:c0 + MOE_SUB, :] = dot(perm, hi).astype(BF16)
                cs_ref[c0:c0 + MOE_SUB, :] = dot(perm, c_hi) + dot(perm, c_mid) + dot(perm, c_lo)
        ys_ref[...] = jnp.zeros(ys_ref.shape, BF16)

    grp = e // EXPERTS_PER_GROUP
    first = meta_ref[grp]
    lane_s = lax.broadcasted_iota(jnp.int32, (MOE_SUB, LANES), 1)

    def sub_tile(si, carry):
        rows = pl.ds(pl.multiple_of((first + si) * MOE_SUB, MOE_SUB), MOE_SUB)
        hb = hs_ref[rows, :]
        hg = dot(hb, wg_ref[...])
        hu = dot(hb, wu_ref[...])
        he = (hg * jax.nn.sigmoid(hg) * hu).astype(BF16)
        ce = jnp.sum(jnp.where(lane_s == e + ROUTE_EXPERT0, cs_ref[rows, :], 0.0), axis=1, keepdims=True)
        ys_ref[rows, :] = (ys_ref[rows, :].astype(F32) + ce * dot(he, wd_ref[...])).astype(BF16)
        return carry

    lax.fori_loop(0, meta_ref[N_GROUPS + grp], sub_tile, 0)

    @pl.when(e == pl.num_programs(1) - 1)
    def _():
        used_rows = meta_ref[2 * N_GROUPS] * MOE_SUB
        for k0 in range(0, MOE_ROWS, MOE_UNSORT_K):
            @pl.when(k0 < used_rows)
            def _(k0=k0):
                c_i = (k0 + lax.broadcasted_iota(jnp.int32, (MOE_CHUNK, MOE_UNSORT_K), 1)).astype(F32)
                ys = ys_ref[k0:k0 + MOE_UNSORT_K, :]
                for c0 in range(0, tm, MOE_CHUNK):
                    unperm = jnp.where(pos_ref[c0:c0 + MOE_CHUNK, 0:1] == c_i, 1.0, 0.0).astype(BF16)
                    o_ref[c0:c0 + MOE_CHUNK, :] += dot(unperm, ys)


def moe_block(x, g, wr_hi, wr_lo, br, w_gate, w_up, w_down):
    T, D = x.shape
    E, _, Fd = w_gate.shape
    tm = MOE_TM
    row = lambda i, e: (i, 0)
    fixed = lambda i, e: (0, 0)
    return pl.pallas_call(
        _moe_kernel,
        grid=(T // tm, E),
        in_specs=[
            pl.BlockSpec((tm, D), row, pipeline_mode=pl.Buffered(1)),
            pl.BlockSpec((1, D), fixed),
            pl.BlockSpec((D, LANES), fixed),
            pl.BlockSpec((D, LANES), fixed),
            pl.BlockSpec((1, LANES), fixed),
            pl.BlockSpec((None, D, Fd), lambda i, e: (e, 0, 0)),
            pl.BlockSpec((None, D, Fd), lambda i, e: (e, 0, 0)),
            pl.BlockSpec((None, Fd, D), lambda i, e: (e, 0, 0)),
        ],
        out_specs=pl.BlockSpec((tm, D), row, pipeline_mode=pl.Buffered(1)),
        out_shape=jax.ShapeDtypeStruct((T, D), F32),
        scratch_shapes=[pltpu.VMEM((MOE_ROWS, D), BF16), pltpu.VMEM((MOE_ROWS, D), BF16),
                        pltpu.VMEM((MOE_ROWS, LANES), F32), pltpu.VMEM((tm, LANES), F32),
                        pltpu.SMEM((2 * N_GROUPS + 1,), jnp.int32)],
        compiler_params=_cparams(("parallel", "arbitrary")),
        name="moe_block",
    )(x, g, wr_hi, wr_lo, br, w_gate, w_up, w_down)


def _prep_route(w_route_group, b_route_group, w_route_expert, b_route_expert):
    D = w_route_group.shape[0]
    pad = LANES - N_GROUPS - N_EXPERTS
    wr = jnp.concatenate([w_route_group, w_route_expert, jnp.zeros((D, pad), F32)], axis=1)
    br = jnp.concatenate([b_route_group, b_route_expert, jnp.zeros((pad,), F32)])[None]
    hi = wr.astype(BF16)
    lo = (wr - hi.astype(F32)).astype(BF16)
    return hi, lo, br


def _prep_lora(w_decay_up, w_aicl_up, w_gate_lora_up):
    zw = jnp.zeros((LANES - DECAY_LORA, RWKV_WIDTH), w_decay_up.dtype)
    wd = jnp.concatenate([w_decay_up, zw], axis=0)
    wa = jnp.concatenate([jnp.zeros((DECAY_LORA, RWKV_WIDTH), w_aicl_up.dtype), w_aicl_up], axis=0)
    wg = jnp.concatenate([w_gate_lora_up,
                          jnp.zeros((2 * LANES - GATE_LORA, RWKV_WIDTH), w_gate_lora_up.dtype)], axis=0)
    return wd.astype(BF16), wa.astype(BF16), wg.astype(BF16)


DSA_COLS = 4176
IN_PROJ_TILES = 2
IN_PROJ_ROWS = 512
RWKV_SPLITS = (RWKV_WIDTH, DECAY_LORA, RWKV_WIDTH, RWKV_WIDTH, AICL_LORA, GATE_LORA)


def _rwkv_cols(t):
    cuts = np.cumsum(RWKV_SPLITS)[:-1].tolist()
    r, dw, k, v, da, dg = jnp.split(t, cuts, axis=-1)
    pad = jnp.zeros(t.shape[:-1] + (RWKV_PAD - sum(RWKV_SPLITS),), t.dtype)
    return jnp.concatenate([r, k, v, dw, da, dg, pad], axis=-1)


def _prep_in_weights(w_in):
    pad = jnp.zeros((w_in.shape[0], DSA_PAD - DSA_COLS), w_in.dtype)
    w_dsa = jnp.concatenate([w_in[:, :DSA_COLS], pad], axis=1)
    w_rwkv = _rwkv_cols(w_in[:, DSA_COLS:])

    def tiles(w):
        d, n = w.shape
        return w.astype(BF16).reshape(d, IN_PROJ_TILES, n // IN_PROJ_TILES).transpose(1, 0, 2)

    return tiles(w_dsa), tiles(w_rwkv)


def kernel(x, g_mix, w_in, rwkv_shift_mix, q_gain, k_gain, w0, w_decay_up, a0, w_aicl_up, w_gate_lora_up, k_k, k_a, r_k, ln_x_w, ln_x_b, w_out, g_ffn, w_route_group, b_route_group, w_route_expert, b_route_expert, w_e_gate, w_e_up, w_e_down):
    B, S, D = x.shape
    T = B * S
    depth = g_mix.shape[0]
    xf = x.reshape(T, D)
    vec = lambda t: t.reshape(1, -1)
    for l in range(depth):
        w_dsa, w_rwkv = _prep_in_weights(w_in[l])
        p_dsa = norm_matmul(xf, vec(g_mix[l]), w_dsa, IN_PROJ_ROWS)
        p_rwkv = norm_matmul(xf, vec(g_mix[l]), w_rwkv, IN_PROJ_ROWS)
        q, k, v, qi, ki, w = dsa_prep(p_dsa, vec(q_gain[l]), vec(k_gain[l]), B, S, DSA_TILE)
        y_dsa = dsa_attention(q, k, v, qi, ki, w, B, S, DSA_TILE, DSA_TILE)
        wd, wa, wg = _prep_lora(w_decay_up[l], w_aicl_up[l], w_gate_lora_up[l])
        y_rwkv = rwkv_mixer(p_rwkv, _rwkv_cols(vec(rwkv_shift_mix[l])), vec(w0[l]), vec(a0[l]),
                            vec(k_k[l]), vec(k_a[l]), vec(r_k[l]), vec(ln_x_w[l]), vec(ln_x_b[l]),
                            wd, wa, wg, B, S, CHUNK)
        xf = out_proj(xf, y_dsa, y_rwkv, w_out[l].astype(BF16), 512)
        wr_hi, wr_lo, br = _prep_route(w_route_group[l], b_route_group[l],
                                       w_route_expert[l], b_route_expert[l])
        xf = moe_block(xf, vec(g_ffn[l]), wr_hi, wr_lo, br, w_e_gate[l].astype(BF16),
                       w_e_up[l].astype(BF16), w_e_down[l].astype(BF16))
    return xf.reshape(B, S, D)
```

```python
import functools

import jax
import jax.numpy as jnp
import numpy as np
from jax import lax
from jax.experimental import pallas as pl
from jax.experimental.pallas import tpu as pltpu

F32 = jnp.float32
BF16 = jnp.bfloat16

NORM_EPS = 1e-6
CHUNK = 64
DSA_HEADS = 8
DSA_HEAD_DIM = 128
DSA_WIDTH = DSA_HEADS * DSA_HEAD_DIM
IDX_HEADS = 16
IDX_DIM = 64
TOPK_MAX = 256
ROPE_THETA = 10000.0
RWKV_HEADS = 16
RWKV_HEAD_DIM = 64
RWKV_WIDTH = RWKV_HEADS * RWKV_HEAD_DIM
DECAY_LORA = 64
AICL_LORA = 64
GATE_LORA = 160
GN_EPS = 64e-5
N_GROUPS = 4
EXPERTS_PER_GROUP = 4
N_EXPERTS = 16
D_EXPERT = 512

LANES = 128
SUBLANES = 8
VMEM_LIMIT = 56 * 1024 * 1024

DSA_PAD = 4352
RWKV_PAD = 3584
DSA_SMALL = 4096
RWKV_SMALL = 3072

NEG_INF = float("-inf")
INT_MIN = -(2 ** 31)
LOG2E = 1.4426950408889634


def _cparams(sem):
    return pltpu.CompilerParams(dimension_semantics=sem, vmem_limit_bytes=VMEM_LIMIT)


def _norm_matmul_kernel(x_ref, g_ref, w_ref, o_ref, h_ref):
    @pl.when(pl.program_id(1) == 0)
    def _():
        x = x_ref[...]
        ms = jnp.mean(x * x, axis=-1, keepdims=True)
        h_ref[...] = (x * lax.rsqrt(ms + NORM_EPS) * g_ref[...]).astype(BF16)

    o_ref[...] = jnp.dot(h_ref[...], w_ref[...], preferred_element_type=F32)


def norm_matmul(x, g, w, tm):
    T, D = x.shape
    nt, _, tn = w.shape
    N = nt * tn
    return pl.pallas_call(
        _norm_matmul_kernel,
        grid=(T // tm, nt),
        in_specs=[
            pl.BlockSpec((tm, D), lambda i, j: (i, 0)),
            pl.BlockSpec((1, D), lambda i, j: (0, 0)),
            pl.BlockSpec((None, D, tn), lambda i, j: (j, 0, 0)),
        ],
        out_specs=pl.BlockSpec((tm, tn), lambda i, j: (i, j)),
        out_shape=jax.ShapeDtypeStruct((T, N), F32),
        scratch_shapes=[pltpu.VMEM((tm, D), BF16)],
        compiler_params=_cparams(("parallel", "arbitrary")),
        name="norm_matmul",
    )(x, g, w)


def _rope_tables(S, d):
    half = d // 2
    inv = ROPE_THETA ** (-jnp.arange(half, dtype=F32) * (2.0 / d))
    ang = jnp.arange(S, dtype=F32)[:, None] * inv[None, :]
    cos, sin = jnp.cos(ang), jnp.sin(ang)
    return jnp.concatenate([cos, cos], axis=-1), jnp.concatenate([-sin, sin], axis=-1)


def _dsa_prep_kernel(p_ref, qg_ref, kg_ref, c128_ref, s128_ref, c64_ref, s64_ref,
                     qt_ref, k_ref, vt_ref, qit_ref, ki_ref, wt_ref):
    c128, s128 = c128_ref[...], s128_ref[...]
    c64, s64 = c64_ref[...], s64_ref[...]
    tm = c128.shape[0]
    first_half = (lax.broadcasted_iota(jnp.int32, (tm, LANES), 1) % IDX_DIM) < IDX_DIM // 2

    def rope128(y):
        return y * c128 + pltpu.roll(y, DSA_HEAD_DIM // 2, 1) * s128

    def rope64(y):
        partner = jnp.where(first_half, pltpu.roll(y, LANES - IDX_DIM // 2, 1),
                            pltpu.roll(y, IDX_DIM // 2, 1))
        return y * c64 + partner * s64

    ones = jnp.ones((LANES, LANES), BF16)
    e_i = lax.broadcasted_iota(jnp.int32, (LANES, LANES), 0)
    e_j = lax.broadcasted_iota(jnp.int32, (LANES, LANES), 1)
    eye = jnp.where(e_i == e_j, 1.0, 0.0).astype(BF16)

    def normed(x, gain):
        sq = x * x
        hi = sq.astype(BF16)
        lo = (sq - hi.astype(F32)).astype(BF16)
        ms = (jnp.dot(hi, ones, preferred_element_type=F32)
              + jnp.dot(lo, ones, preferred_element_type=F32)) * (1.0 / DSA_HEAD_DIM)
        return x * lax.rsqrt(ms + NORM_EPS) * gain

    def transposed(y):
        return _nt_dot(eye, y.astype(BF16)).astype(BF16)

    scale = DSA_HEAD_DIM ** -0.5 * LOG2E
    for h in range(DSA_HEADS):
        lo = h * DSA_HEAD_DIM
        q = p_ref[:, lo:lo + DSA_HEAD_DIM]
        qt_ref[lo:lo + DSA_HEAD_DIM, :] = transposed(rope128(normed(q, qg_ref[...])) * scale)
        k = p_ref[:, DSA_WIDTH + lo:DSA_WIDTH + lo + DSA_HEAD_DIM]
        k_ref[:, lo:lo + DSA_HEAD_DIM] = rope128(normed(k, kg_ref[...])).astype(BF16)
        v = p_ref[:, 2 * DSA_WIDTH + lo:2 * DSA_WIDTH + lo + DSA_HEAD_DIM]
        vt_ref[lo:lo + DSA_HEAD_DIM, :] = transposed(v)
    for j in range(IDX_HEADS * IDX_DIM // LANES):
        lo = 3 * DSA_WIDTH + j * LANES
        qit_ref[j * LANES:(j + 1) * LANES, :] = transposed(rope64(p_ref[:, lo:lo + LANES]))
    small = p_ref[:, DSA_SMALL:DSA_SMALL + LANES]
    lane = lax.broadcasted_iota(jnp.int32, (tm, LANES), 1)
    ki = rope64(small)
    ki_ref[:, :LANES] = jnp.where(lane < IDX_DIM, ki, 0.0).astype(BF16)
    ki_ref[:, LANES:] = jnp.where(lane >= IDX_DIM, pltpu.roll(ki, IDX_DIM, 1), 0.0).astype(BF16)
    wt_ref[...] = (small * (IDX_HEADS ** -0.5 * IDX_DIM ** -0.5)).T


def dsa_prep(p_dsa, q_gain, k_gain, B, S, tm):
    T = B * S
    c128, s128 = _rope_tables(S, DSA_HEAD_DIM)
    c64, s64 = _rope_tables(S, IDX_DIM)
    c64 = jnp.concatenate([c64, c64], axis=-1)
    s64 = jnp.concatenate([s64, s64], axis=-1)
    nsb = S // tm
    row = lambda i: (i, 0)
    pos = lambda i: (i % nsb, 0)
    fixed = lambda i: (0, 0)
    tile_t = lambda i: (i // nsb, i % nsb, 0, 0)
    wide_t = pl.BlockSpec((None, None, DSA_WIDTH, tm), tile_t)
    return pl.pallas_call(
        _dsa_prep_kernel,
        grid=(T // tm,),
        in_specs=[
            pl.BlockSpec((tm, DSA_PAD), row),
            pl.BlockSpec((1, DSA_HEAD_DIM), fixed),
            pl.BlockSpec((1, DSA_HEAD_DIM), fixed),
            pl.BlockSpec((tm, LANES), pos),
            pl.BlockSpec((tm, LANES), pos),
            pl.BlockSpec((tm, LANES), pos),
            pl.BlockSpec((tm, LANES), pos),
        ],
        out_specs=[
            wide_t,
            pl.BlockSpec((tm, DSA_WIDTH), row),
            wide_t,
            wide_t,
            pl.BlockSpec((tm, 2 * LANES), row),
            pl.BlockSpec((None, None, LANES, tm), tile_t),
        ],
        out_shape=[
            jax.ShapeDtypeStruct((B, nsb, DSA_WIDTH, tm), BF16),
            jax.ShapeDtypeStruct((T, DSA_WIDTH), BF16),
            jax.ShapeDtypeStruct((B, nsb, DSA_WIDTH, tm), BF16),
            jax.ShapeDtypeStruct((B, nsb, IDX_HEADS * IDX_DIM, tm), BF16),
            jax.ShapeDtypeStruct((T, 2 * LANES), BF16),
            jax.ShapeDtypeStruct((B, nsb, LANES, tm), F32),
        ],
        compiler_params=_cparams(("parallel",)),
        name="dsa_prep",
    )(p_dsa, q_gain, k_gain, c128, s128, c64, s64)


SCORE_ROWS = 128
DSA_TILE = 256
SEARCH_BLIND = 18
SEARCH_CAP = 40
PEEL_FROM = 16
INF = float("inf")
SHIFT_BOUND_MAX = 50.0
ONES_ROWS = 16


def _nt_dot(a, b):
    return lax.dot_general(a, b, (((1,), (1,)), ((), ())), preferred_element_type=F32)


def _dsa_kernel(qt_ref, k_ref, vt_ref, qit_ref, ki_ref, wt_ref, y_ref,
                sc_ref, bias_ref, thr_ref, m_ref, l_ref, acc_ref, ksq_ref,
                s0_ref, s1_ref, p0_ref, p1_ref, *, tq, tk, top_k):
    i = pl.program_id(1)
    nkb = ((i + 1) * tq + tk - 1) // tk
    qpos = i * tq + lax.broadcasted_iota(jnp.int32, (1, tq), 1)
    limit = (qpos // CHUNK + 1) * CHUNK
    kf = float(top_k)

    def blk(kb):
        return pl.ds(pl.multiple_of(kb * tk, tk), tk)

    def key_pos(kb, rows, r0=0):
        return kb * tk + r0 + lax.broadcasted_iota(jnp.int32, (rows, tq), 0)

    def score_block(kb, carry):
        mn, mx = carry
        for r0 in range(0, tk, SCORE_ROWS):
            kblk = ki_ref[pl.ds(pl.multiple_of(kb * tk, tk) + r0, SCORE_ROWS), :]
            acc = jnp.zeros((SCORE_ROWS, tq), F32)
            for j in range(IDX_HEADS // 2):
                qblk = qit_ref[j * LANES:(j + 1) * LANES, :]
                s_even = jnp.dot(kblk[:, :LANES], qblk, preferred_element_type=F32)
                s_odd = jnp.dot(kblk[:, LANES:], qblk, preferred_element_type=F32)
                acc = acc + jnp.maximum(s_even, 0.0) * wt_ref[IDX_DIM + 2 * j:IDX_DIM + 2 * j + 1, :]
                acc = acc + jnp.maximum(s_odd, 0.0) * wt_ref[IDX_DIM + 2 * j + 1:IDX_DIM + 2 * j + 2, :]
            adm = key_pos(kb, SCORE_ROWS, r0) < limit
            sc_ref[pl.ds(pl.multiple_of(kb * tk, tk) + r0, SCORE_ROWS), :] = jnp.where(adm, acc, NEG_INF)
            mn = jnp.minimum(mn, jnp.min(jnp.where(adm, acc, INF), axis=0, keepdims=True))
            mx = jnp.maximum(mx, jnp.max(jnp.where(adm, acc, NEG_INF), axis=0, keepdims=True))
        return mn, mx

    row_min, row_max = lax.fori_loop(
        0, nkb, score_block, (jnp.full((1, tq), INF, F32), jnp.full((1, tq), NEG_INF, F32)))

    def fold8(x, op):
        return op(x.reshape(x.shape[0] // SUBLANES, SUBLANES, tq), axis=0)

    def count_where(pred):
        def body(kb, acc):
            return acc + fold8(jnp.where(pred(sc_ref[blk(kb), :]), 1.0, 0.0), jnp.sum)
        part = lax.fori_loop(0, nkb, body, jnp.zeros((SUBLANES, tq), F32))
        return jnp.sum(part, axis=0, keepdims=True)

    def count_ge(x):
        return count_where(lambda s: s >= x)

    def max_below(x):
        def body(kb, acc):
            s = sc_ref[blk(kb), :]
            return jnp.maximum(acc, fold8(jnp.where(s < x, s, NEG_INF), jnp.max))
        part = lax.fori_loop(0, nkb, body, jnp.full((SUBLANES, tq), NEG_INF, F32))
        return jnp.max(part, axis=0, keepdims=True)

    take_all = limit <= top_k

    def search_cond(st):
        it, done = st[0], st[1]
        return jnp.logical_and(it < SEARCH_CAP, jnp.min(done) < 0.5)

    def search_step(st, may_peel=True):
        it, done, thr, lo, hi = st
        mid = 0.5 * lo + 0.5 * hi
        peel = jnp.logical_and(it >= PEEL_FROM, it % 4 == 3) if may_peel else False
        if may_peel:
            mid = lax.cond(peel, lambda: max_below(hi), lambda: mid)
        c = count_ge(mid)
        active = done < 0.5
        hit = jnp.logical_or(c == kf, jnp.logical_and(peel, c >= kf))
        fin = jnp.logical_and(active, hit)
        up = jnp.logical_and(active, jnp.logical_and(jnp.logical_not(hit), c > kf))
        dn = jnp.logical_and(active, c < kf)
        return (it + 1, jnp.where(fin, 1.0, done), jnp.where(fin, mid, thr),
                jnp.where(up, mid, lo), jnp.where(dn, mid, hi))

    done0 = jnp.where(take_all, 1.0, 0.0)
    st = (jnp.int32(0), done0, jnp.full((1, tq), NEG_INF, F32), row_min,
          row_max + (jnp.abs(row_max) * 2.0 ** -10 + 1e-30))
    st = lax.fori_loop(0, SEARCH_BLIND, lambda _, s: search_step(s, may_peel=False), st)
    st = lax.while_loop(search_cond, search_step, st)
    done, thr = st[1], st[2]
    thr_ref[...] = thr

    @pl.when(jnp.min(done) < 0.5)
    def _():
        def from_key(key):
            return pltpu.bitcast(jnp.where(key < 0, key ^ jnp.int32(0x7FFFFFFF), key), F32)

        def bit_step(b, key):
            cand = key ^ jnp.left_shift(jnp.int32(1), 31 - b)
            return jnp.where(count_ge(from_key(cand)) >= kf, cand, key)

        key = lax.fori_loop(0, 32, bit_step, jnp.full((1, tq), INT_MIN, jnp.int32))
        thr_ref[...] = jnp.where(done < 0.5, from_key(key), thr)

    thr = thr_ref[...]
    tied = jnp.logical_and(count_ge(thr) > kf, jnp.logical_not(take_all))
    any_tied = jnp.max(jnp.where(tied, 1.0, 0.0)) > 0.5

    @pl.when(jnp.logical_not(any_tied))
    def _():
        def body(kb, carry):
            sel = jnp.logical_and(sc_ref[blk(kb), :] >= thr, key_pos(kb, tk) < limit)
            bias_ref[blk(kb), :] = jnp.where(sel, 0.0, NEG_INF)
            return carry
        lax.fori_loop(0, nkb, body, 0)

    @pl.when(any_tied)
    def _():
        need = kf - count_where(lambda s: s > thr)
        r_i = lax.broadcasted_iota(jnp.int32, (tk, tk), 0)
        c_i = lax.broadcasted_iota(jnp.int32, (tk, tk), 1)
        lower = jnp.where(c_i <= r_i, 1.0, 0.0).astype(BF16)

        def body(kb, seen):
            s = sc_ref[blk(kb), :]
            eq = s == thr
            prefix = seen + jnp.dot(lower, jnp.where(eq, 1.0, 0.0).astype(BF16),
                                    preferred_element_type=F32)
            sel = jnp.logical_or(s > thr, jnp.logical_and(eq, prefix <= need))
            sel = jnp.logical_or(sel, take_all)
            sel = jnp.logical_and(sel, key_pos(kb, tk) < limit)
            bias_ref[blk(kb), :] = jnp.where(sel, 0.0, NEG_INF)
            return prefix[tk - 1:tk, :]
        lax.fori_loop(0, nkb, body, jnp.zeros((1, tq), F32))

    heads = [slice(h * DSA_HEAD_DIM, (h + 1) * DSA_HEAD_DIM) for h in range(DSA_HEADS)]

    @pl.when(i == 0)
    def _():
        def body(kb, carry):
            out = []
            for h, hs in enumerate(heads):
                kh = k_ref[blk(kb), hs].astype(F32)
                sq = jnp.max(jnp.sum(kh * kh, axis=1, keepdims=True), axis=0, keepdims=True)
                out.append(jnp.maximum(carry[h], sq))
            return tuple(out)
        ksq = lax.fori_loop(0, k_ref.shape[0] // tk, body,
                            tuple(jnp.zeros((1, 1), F32) for _ in heads))
        for h in range(DSA_HEADS):
            ksq_ref[h] = jnp.broadcast_to(ksq[h], (1, tq))

    worst = jnp.zeros((1, tq), F32)
    for h, hs in enumerate(heads):
        qh = qt_ref[hs, :].astype(F32)
        bound = jnp.sqrt(jnp.sum(qh * qh, axis=0, keepdims=True) * ksq_ref[h])
        m_ref[h] = bound
        worst = jnp.maximum(worst, bound)
    shift_ok = jnp.max(worst) <= SHIFT_BOUND_MAX
    acc_ref[...] = jnp.zeros(acc_ref.shape, F32)

    @pl.when(shift_ok)
    def _():
        ones = jnp.ones((ONES_ROWS, tk), BF16)
        n_blocks = k_ref.shape[0] // tk
        bias_ref[pl.ds(n_blocks * tk, tk), :] = jnp.full((tk, tq), NEG_INF, F32)
        s1_ref[...] = jnp.zeros(s1_ref.shape, F32)
        p0_ref[...] = jnp.zeros(p0_ref.shape, BF16)

        def step(t, s_w, s_r, p_w, p_r):
            ta = jnp.minimum(t, nkb - 1)
            for h, hs in enumerate(heads):
                s_w[h] = jnp.dot(k_ref[blk(ta), hs], qt_ref[hs, :], preferred_element_type=F32)
            tb = t - 1
            valid = jnp.logical_and(tb >= 0, tb < nkb)
            bias = bias_ref[blk(jnp.where(valid, tb, n_blocks)), :]
            for h in range(DSA_HEADS):
                p_w[h] = jnp.exp2(s_r[h] + bias - m_ref[h]).astype(BF16)
            tc = jnp.maximum(t - 2, 0)
            for h, hs in enumerate(heads):
                acc_ref[h] += jnp.dot(jnp.concatenate([vt_ref[tc, hs, :], ones], axis=0), p_r[h],
                                      preferred_element_type=F32)

        def one_step(t, carry):
            @pl.when(t % 2 == 0)
            def _():
                step(t, s0_ref, s1_ref, p1_ref, p0_ref)

            @pl.when(t % 2 == 1)
            def _():
                step(t, s1_ref, s0_ref, p0_ref, p1_ref)
            return carry

        lax.fori_loop(0, nkb + 2, one_step, 0)
        for h, hs in enumerate(heads):
            acc = acc_ref[h]
            y = acc[:DSA_HEAD_DIM] / acc[DSA_HEAD_DIM:DSA_HEAD_DIM + 1]
            y_ref[:, hs] = y.T.astype(y_ref.dtype)

    @pl.when(jnp.logical_not(shift_ok))
    def _():
        m_ref[...] = jnp.full(m_ref.shape, NEG_INF, F32)
        l_ref[...] = jnp.zeros(l_ref.shape, F32)

        def attend(kb, carry):
            bias = bias_ref[blk(kb), :]
            for h, hs in enumerate(heads):
                s = jnp.dot(k_ref[blk(kb), hs], qt_ref[hs, :], preferred_element_type=F32) + bias
                m = m_ref[h]
                m_new = jnp.maximum(m, jnp.max(s, axis=0, keepdims=True))
                m_use = jnp.where(m_new == NEG_INF, 0.0, m_new)
                p = jnp.exp2(s - m_use)
                alpha = jnp.exp2(m - m_use)
                m_ref[h] = m_new
                l_ref[h] = alpha * l_ref[h] + jnp.sum(p, axis=0, keepdims=True)
                acc_ref[h, :DSA_HEAD_DIM, :] = (
                    alpha * acc_ref[h, :DSA_HEAD_DIM, :]
                    + jnp.dot(vt_ref[kb, hs, :], p.astype(BF16), preferred_element_type=F32))
            return carry

        lax.fori_loop(0, nkb, attend, 0)
        for h, hs in enumerate(heads):
            y_ref[:, hs] = (acc_ref[h, :DSA_HEAD_DIM, :] / l_ref[h]).T.astype(y_ref.dtype)


def dsa_attention(qt, k, vt, qit, ki, wt, B, S, tq, tk):
    assert tq == tk and qt.shape[-1] == tq
    T = B * S
    nq = S // tq
    top_k = min(TOPK_MAX, S // 4)
    qtile = lambda b, i: (b, i, 0, 0)
    seq = lambda b, i: (b, 0)
    return pl.pallas_call(
        functools.partial(_dsa_kernel, tq=tq, tk=tk, top_k=top_k),
        grid=(B, nq),
        in_specs=[
            pl.BlockSpec((None, None, DSA_WIDTH, tq), qtile),
            pl.BlockSpec((S, DSA_WIDTH), seq),
            pl.BlockSpec((None, nq, DSA_WIDTH, tk), lambda b, i: (b, 0, 0, 0)),
            pl.BlockSpec((None, None, IDX_HEADS * IDX_DIM, tq), qtile),
            pl.BlockSpec((S, 2 * LANES), seq),
            pl.BlockSpec((None, None, LANES, tq), qtile),
        ],
        out_specs=pl.BlockSpec((tq, DSA_WIDTH), lambda b, i: (b * nq + i, 0)),
        out_shape=jax.ShapeDtypeStruct((T, DSA_WIDTH), BF16),
        scratch_shapes=[pltpu.VMEM((S, tq), F32), pltpu.VMEM((S + tk, tq), F32),
                        pltpu.VMEM((1, tq), F32),
                        pltpu.VMEM((DSA_HEADS, 1, tq), F32), pltpu.VMEM((DSA_HEADS, 1, tq), F32),
                        pltpu.VMEM((DSA_HEADS, DSA_HEAD_DIM + ONES_ROWS, tq), F32),
                        pltpu.VMEM((DSA_HEADS, 1, tq), F32),
                        pltpu.VMEM((DSA_HEADS, tk, tq), F32), pltpu.VMEM((DSA_HEADS, tk, tq), F32),
                        pltpu.VMEM((DSA_HEADS, tk, tq), BF16), pltpu.VMEM((DSA_HEADS, tk, tq), BF16)],
        compiler_params=_cparams(("parallel", "arbitrary")),
        name="dsa_attention",
    )(qt, k, vt, qit, ki, wt)


RW_PAIRS = RWKV_HEADS // 2
RW_GROUPS = RWKV_HEADS // 4


def _split3(x):
    hi = x.astype(BF16)
    r1 = x - hi.astype(F32)
    mid = r1.astype(BF16)
    lo = (r1 - mid.astype(F32)).astype(BF16)
    return hi, mid, lo


def _rwkv_kernel(p_ref, mu_ref, w0_ref, a0_ref, kk_ref, ka_ref, rk_ref, lnw_ref, lnb_ref,
                 wd_ref, wa_ref, wg_ref, y_ref, state_ref, prev_ref, *, C):
    c = pl.program_id(1)

    @pl.when(c == 0)
    def _():
        state_ref[...] = jnp.zeros_like(state_ref)
        prev_ref[...] = jnp.zeros_like(prev_ref)

    W = RWKV_WIDTH
    dot = functools.partial(jnp.dot, preferred_element_type=F32)

    p = p_ref[...]
    first_row = lax.broadcasted_iota(jnp.int32, p.shape, 0) == 0
    prev = jnp.where(first_row, prev_ref[...], pltpu.roll(p, 1, 0))
    prev_ref[...] = p[C - 1:C, :]
    xs = p + (prev - p) * mu_ref[...]

    small = xs[:, RWKV_SMALL:RWKV_SMALL + LANES]
    gate_in = xs[:, RWKV_SMALL + LANES:RWKV_SMALL + 3 * LANES]
    z = w0_ref[...] + dot(jnp.tanh(small).astype(BF16), wd_ref[...])
    softplus = jnp.maximum(-z, 0.0) + jnp.log(1.0 + jnp.exp(-jnp.abs(z)))
    logdec = -jnp.exp(-softplus - 0.5)
    rate = jax.nn.sigmoid(a0_ref[...] + dot(small.astype(BF16), wa_ref[...]))
    gate = dot(jax.nn.sigmoid(gate_in).astype(BF16), wg_ref[...])

    ti = lax.broadcasted_iota(jnp.int32, (C, C), 0)
    tj = lax.broadcasted_iota(jnp.int32, (C, C), 1)
    lower = jnp.where(tj <= ti, 1.0, 0.0).astype(BF16)
    hi, mid, lo = _split3(logdec)
    cum = dot(lower, hi) + dot(lower, mid) + dot(lower, lo)

    gi = lax.broadcasted_iota(jnp.int32, (LANES, LANES), 0) // RWKV_HEAD_DIM
    gj = lax.broadcasted_iota(jnp.int32, (LANES, LANES), 1) // RWKV_HEAD_DIM
    same_head = jnp.where(gi == gj, 1.0, 0.0).astype(BF16)

    def head_sum(parts):
        x = jnp.concatenate(parts, axis=0)
        xh = x.astype(BF16)
        xl = (x - xh.astype(F32)).astype(BF16)
        s = dot(xh, same_head) + dot(xl, same_head)
        return [s[n * C:(n + 1) * C] for n in range(len(parts))]

    lane = lax.broadcasted_iota(jnp.int32, (C, LANES), 1)
    even = lane < RWKV_HEAD_DIM
    zeros = jnp.zeros((C, LANES), F32)

    pairs = []
    for pi in range(RW_PAIRS):
        sl = slice(pi * LANES, (pi + 1) * LANES)
        r = xs[:, sl]
        k = xs[:, W + pi * LANES:W + (pi + 1) * LANES]
        v = xs[:, 2 * W + pi * LANES:2 * W + (pi + 1) * LANES]
        a = rate[:, sl]
        kk = k * kk_ref[:, sl]
        k2 = k * (1.0 + (a - 1.0) * ka_ref[:, sl])
        pairs.append(dict(r=r, v=v, a=a, kk=kk, k2=k2, sl=sl))
    sums = head_sum([d["kk"] * d["kk"] for d in pairs] + [d["r"] * d["k2"] * rk_ref[:, d["sl"]] for d in pairs])
    for pi, d in enumerate(pairs):
        sl = d["sl"]
        kkn = d["kk"] / jnp.maximum(jnp.sqrt(sums[pi]), 1e-12)
        d["bonus"] = sums[RW_PAIRS + pi] * d["v"]
        b = kkn * d["a"]
        cm = cum[:, sl]
        last = cm[C - 1:C, :]
        e_neg = jnp.exp(-cm)
        e_rem = jnp.exp(last - cm)
        d["At"] = -kkn * jnp.exp(cm - logdec[:, sl])
        d["Rt"] = d["r"] * jnp.exp(cm)
        d["Bt"] = b * e_neg
        d["Kt"] = d["k2"] * e_neg
        d["Bh"] = b * e_rem
        d["Kh"] = d["k2"] * e_rem
        d["wlast"] = jnp.exp(last)

    G = 4 * C
    ri = lax.broadcasted_iota(jnp.int32, (G, G), 0)
    ci = lax.broadcasted_iota(jnp.int32, (G, G), 1)
    same_block = (ri // C) == (ci // C)
    strict = jnp.logical_and(same_block, ci < ri)
    incl = jnp.logical_and(same_block, ci <= ri)
    pair_block = jnp.where(gi == gj, 1.0, 0.0)

    def masked4(x0, x1):
        return jnp.concatenate([
            jnp.concatenate([jnp.where(even, x0, 0.0), zeros], axis=1),
            jnp.concatenate([jnp.where(even, 0.0, x0), zeros], axis=1),
            jnp.concatenate([zeros, jnp.where(even, x1, 0.0)], axis=1),
            jnp.concatenate([zeros, jnp.where(even, 0.0, x1)], axis=1)], axis=0).astype(BF16)

    def plain4(x0, x1):
        return jnp.concatenate([
            jnp.concatenate([x0, zeros], axis=1), jnp.concatenate([x0, zeros], axis=1),
            jnp.concatenate([zeros, x1], axis=1), jnp.concatenate([zeros, x1], axis=1)],
            axis=0).astype(BF16)

    groups = []
    for g in range(RW_GROUPS):
        d0, d1 = pairs[2 * g], pairs[2 * g + 1]
        lhs_a = masked4(d0["At"], d1["At"])
        lhs_r = masked4(d0["Rt"], d1["Rt"])
        rhs_b = plain4(d0["Bt"], d1["Bt"])
        rhs_k = plain4(d0["Kt"], d1["Kt"])
        groups.append(dict(
            d=(d0, d1),
            m=jnp.where(strict, _nt_dot(lhs_a, rhs_b), 0.0).astype(BF16),
            m_ak=jnp.where(strict, _nt_dot(lhs_a, rhs_k), 0.0).astype(BF16),
            r_b=jnp.where(incl, _nt_dot(lhs_r, rhs_b), 0.0).astype(BF16),
            r_k=jnp.where(incl, _nt_dot(lhs_r, rhs_k), 0.0).astype(BF16),
            v4=jnp.concatenate([d0["v"], d0["v"], d1["v"], d1["v"]], axis=0).astype(BF16)))
    for g, grp in enumerate(groups):
        a0s, y0s = [], []
        for n, d in enumerate(grp["d"]):
            st = state_ref[2 * g + n].astype(BF16)
            both = _nt_dot(jnp.concatenate([d["At"], d["Rt"]], axis=0).astype(BF16), st)
            a0s.append(both[:C])
            y0s.append(both[C:])
        grp["y0"] = y0s
        grp["x"] = (jnp.concatenate([a0s[0], a0s[0], a0s[1], a0s[1]], axis=0)
                    + dot(grp["m_ak"], grp["v4"]))
    steps = max(1, int(np.ceil(np.log2(C))))
    for s in range(steps):
        for grp in groups:
            grp["x"] = grp["x"] + dot(grp["m"], grp["x"].astype(BF16))
        if s + 1 < steps:
            for grp in groups:
                grp["m"] = dot(grp["m"], grp["m"]).astype(BF16)
    ys = []
    for g, grp in enumerate(groups):
        x = grp["x"]
        yc = dot(grp["r_b"], x.astype(BF16)) + dot(grp["r_k"], grp["v4"])
        for n, d in enumerate(grp["d"]):
            lo_, hi_ = 2 * n * C, (2 * n + 1) * C
            y = grp["y0"][n] + jnp.where(even, yc[lo_:hi_], yc[hi_:hi_ + C])
            u = jnp.where(even, x[lo_:hi_], x[hi_:hi_ + C])
            uv = jnp.concatenate([u, d["v"]], axis=0).astype(BF16)
            bk = jnp.concatenate([d["Bh"], d["Kh"]], axis=0).astype(BF16)
            upd = lax.dot_general(uv, bk, (((0,), (0,)), ((), ())), preferred_element_type=F32)
            state_ref[2 * g + n] = state_ref[2 * g + n] * d["wlast"] + upd * pair_block
            ys.append(y)

    means = head_sum(ys)
    devs = [y - m_ * (1.0 / RWKV_HEAD_DIM) for y, m_ in zip(ys, means)]
    varis = head_sum([dv * dv for dv in devs])
    for pi, d in enumerate(pairs):
        sl = d["sl"]
        yn = devs[pi] * lax.rsqrt(varis[pi] * (1.0 / RWKV_HEAD_DIM) + GN_EPS)
        yn = yn * lnw_ref[:, sl] + lnb_ref[:, sl]
        y_ref[:, sl] = ((yn + d["bonus"]) * gate[:, sl]).astype(y_ref.dtype)


def rwkv_mixer(p_rwkv, mu, w0, a0, k_k, k_a, r_k, ln_w, ln_b, wd, wa, wg, B, S, C):
    T = B * S
    nc = S // C
    vec = pl.BlockSpec((1, RWKV_WIDTH), lambda b, c: (0, 0))
    return pl.pallas_call(
        functools.partial(_rwkv_kernel, C=C),
        grid=(B, nc),
        in_specs=[
            pl.BlockSpec((C, RWKV_PAD), lambda b, c: (b * nc + c, 0)),
            pl.BlockSpec((1, RWKV_PAD), lambda b, c: (0, 0)),
            vec, vec, vec, vec, vec, vec, vec,
            pl.BlockSpec((LANES, RWKV_WIDTH), lambda b, c: (0, 0)),
            pl.BlockSpec((LANES, RWKV_WIDTH), lambda b, c: (0, 0)),
            pl.BlockSpec((2 * LANES, RWKV_WIDTH), lambda b, c: (0, 0)),
        ],
        out_specs=pl.BlockSpec((C, RWKV_WIDTH), lambda b, c: (b * nc + c, 0)),
        out_shape=jax.ShapeDtypeStruct((T, RWKV_WIDTH), BF16),
        scratch_shapes=[pltpu.VMEM((RW_PAIRS, LANES, LANES), F32),
                        pltpu.VMEM((1, RWKV_PAD), F32)],
        compiler_params=_cparams(("parallel", "arbitrary")),
        name="rwkv_mixer",
    )(p_rwkv, mu, w0, a0, k_k, k_a, r_k, ln_w, ln_b, wd, wa, wg)


def _out_proj_kernel(x_ref, ya_ref, yb_ref, wa_ref, wb_ref, o_ref):
    o_ref[...] = (x_ref[...]
                  + jnp.dot(ya_ref[...], wa_ref[...], preferred_element_type=F32)
                  + jnp.dot(yb_ref[...], wb_ref[...], preferred_element_type=F32))


def out_proj(x, ya, yb, w_out, tm):
    T, D = x.shape
    half = ya.shape[1]
    row = lambda i: (i, 0)
    return pl.pallas_call(
        _out_proj_kernel,
        grid=(T // tm,),
        in_specs=[
            pl.BlockSpec((tm, D), row),
            pl.BlockSpec((tm, half), row),
            pl.BlockSpec((tm, half), row),
            pl.BlockSpec((half, D), lambda i: (0, 0)),
            pl.BlockSpec((half, D), lambda i: (1, 0)),
        ],
        out_specs=pl.BlockSpec((tm, D), row),
        out_shape=jax.ShapeDtypeStruct((T, D), F32),
        compiler_params=_cparams(("parallel",)),
        name="out_proj",
    )(x, ya, yb, w_out, w_out)


ROUTE_EXPERT0 = N_GROUPS


def _route(logits):
    lane = lax.broadcasted_iota(jnp.int32, logits.shape, 1)
    big = jnp.int32(LANES)
    is_grp = lane < N_GROUPS
    gl = jnp.where(is_grp, logits, NEG_INF)
    gmax = jnp.max(gl, axis=1, keepdims=True)
    p_grp = 1.0 / jnp.sum(jnp.exp(gl - gmax), axis=1, keepdims=True)
    g_sel = jnp.min(jnp.where(gl == gmax, lane, big), axis=1, keepdims=True)
    lo = ROUTE_EXPERT0 + g_sel * EXPERTS_PER_GROUP
    in_grp = jnp.logical_and(lane >= lo, lane < lo + EXPERTS_PER_GROUP)
    el = jnp.where(in_grp, logits, NEG_INF)
    ee = jnp.exp(el - jnp.max(el, axis=1, keepdims=True))
    pe = jnp.where(in_grp, ee / jnp.sum(ee, axis=1, keepdims=True), -1.0)
    p1 = jnp.max(pe, axis=1, keepdims=True)
    i1 = jnp.min(jnp.where(pe == p1, lane, big), axis=1, keepdims=True)
    pe2 = jnp.where(lane == i1, -1.0, pe)
    p2 = jnp.max(pe2, axis=1, keepdims=True)
    i2 = jnp.min(jnp.where(pe2 == p2, lane, big), axis=1, keepdims=True)
    tot = p1 + p2
    comb = jnp.where(lane == i1, p_grp * (p1 / tot), jnp.where(lane == i2, p_grp * (p2 / tot), 0.0))
    return comb, g_sel


MOE_TM = 1024
MOE_SUB = 288
MOE_CHUNK = 256
MOE_PAD = (MOE_TM + N_GROUPS * (MOE_SUB - 1)) // MOE_SUB * MOE_SUB
MOE_UNSORT_K = 512
MOE_ROWS = -(-MOE_PAD // MOE_UNSORT_K) * MOE_UNSORT_K


def _moe_kernel(x_ref, g_ref, wr_hi_ref, wr_lo_ref, br_ref, wg_ref, wu_ref, wd_ref, o_ref,
                hs_ref, ys_ref, cs_ref, pos_ref, meta_ref):
    e = pl.program_id(1)
    tm = x_ref.shape[0]
    dot = functools.partial(jnp.dot, preferred_element_type=F32)

    @pl.when(e == 0)
    def _():
        x = x_ref[...]
        ms = jnp.mean(x * x, axis=-1, keepdims=True)
        h = x * lax.rsqrt(ms + NORM_EPS) * g_ref[...]
        o_ref[...] = x
        hi = h.astype(BF16)
        lo = (h - hi.astype(F32)).astype(BF16)
        logits = dot(hi, wr_hi_ref[...]) + dot(lo, wr_hi_ref[...]) + dot(hi, wr_lo_ref[...])
        comb, g_sel = _route(logits + br_ref[...])

        lane = lax.broadcasted_iota(jnp.int32, (tm, LANES), 1)
        onehot = jnp.where(lane == g_sel, 1.0, 0.0)
        onehot_b = onehot.astype(BF16)
        ranks = []
        for c0 in range(0, tm, MOE_CHUNK):
            t_i = c0 + lax.broadcasted_iota(jnp.int32, (MOE_CHUNK, tm), 0)
            t_j = lax.broadcasted_iota(jnp.int32, (MOE_CHUNK, tm), 1)
            ranks.append(dot(jnp.where(t_j < t_i, 1.0, 0.0).astype(BF16), onehot_b))
        rank = jnp.concatenate(ranks, axis=0)
        counts = jnp.sum(onehot, axis=0, keepdims=True)
        padded = jnp.floor((counts + (MOE_SUB - 0.5)) * (1.0 / MOE_SUB)) * MOE_SUB
        lane1 = lax.broadcasted_iota(jnp.int32, (1, LANES), 1)
        base = jnp.zeros((1, LANES), F32)
        start = jnp.zeros((1, 1), F32)
        for g in range(N_GROUPS):
            pad_g = jnp.sum(jnp.where(lane1 == g, padded, 0.0), axis=1, keepdims=True)
            base = jnp.where(lane1 == g, start, base)
            meta_ref[g] = (start[0, 0] * (1.0 / MOE_SUB) + 0.5).astype(jnp.int32)
            meta_ref[N_GROUPS + g] = (pad_g[0, 0] * (1.0 / MOE_SUB) + 0.5).astype(jnp.int32)
            start = start + pad_g
        pos = jnp.sum(onehot * (rank + base), axis=1, keepdims=True)
        pos_b = jnp.broadcast_to(pos, (tm, LANES))
        pos_ref[...] = pos_b
        pos_row = pos_b.T[0:1, :]
        n_used = (start[0, 0] * (1.0 / MOE_SUB) + 0.5).astype(jnp.int32)
        meta_ref[2 * N_GROUPS] = n_used
        c_hi, c_mid, c_lo = _split3(comb)
        for c in range(MOE_PAD // MOE_SUB):
            @pl.when(c < n_used)
            def _(c0=c * MOE_SUB):
                r_i = (c0 + lax.broadcasted_iota(jnp.int32, (MOE_SUB, tm), 0)).astype(F32)
                perm = jnp.where(r_i == pos_row, 1.0, 0.0).astype(BF16)
                hs_ref[c0:c0 + MOE_SUB, :] = dot(perm, hi).astype(BF16)
                cs_ref[c0:c0 + MOE_SUB, :] = dot(perm, c_hi) + dot(perm, c_mid) + dot(perm, c_lo)
        ys_ref[...] = jnp.zeros(ys_ref.shape, BF16)

    grp = e // EXPERTS_PER_GROUP
    first = meta_ref[grp]
    lane_s = lax.broadcasted_iota(jnp.int32, (MOE_SUB, LANES), 1)

    def sub_tile(si, carry):
        rows = pl.ds(pl.multiple_of((first + si) * MOE_SUB, MOE_SUB), MOE_SUB)
        hb = hs_ref[rows, :]
        hg = dot(hb, wg_ref[...])
        hu = dot(hb, wu_ref[...])
        he = (hg * jax.nn.sigmoid(hg) * hu).astype(BF16)
        ce = jnp.sum(jnp.where(lane_s == e + ROUTE_EXPERT0, cs_ref[rows, :], 0.0), axis=1, keepdims=True)
        ys_ref[rows, :] = (ys_ref[rows, :].astype(F32) + ce * dot(he, wd_ref[...])).astype(BF16)
        return carry

    lax.fori_loop(0, meta_ref[N_GROUPS + grp], sub_tile, 0)

    @pl.when(e == pl.num_programs(1) - 1)
    def _():
        used_rows = meta_ref[2 * N_GROUPS] * MOE_SUB
        for k0 in range(0, MOE_ROWS, MOE_UNSORT_K):
            @pl.when(k0 < used_rows)
            def _(k0=k0):
                c_i = (k0 + lax.broadcasted_iota(jnp.int32, (MOE_CHUNK, MOE_UNSORT_K), 1)).astype(F32)
                ys = ys_ref[k0:k0 + MOE_UNSORT_K, :]
                for c0 in range(0, tm, MOE_CHUNK):
                    unperm = jnp.where(pos_ref[c0:c0 + MOE_CHUNK, 0:1] == c_i, 1.0, 0.0).astype(BF16)
                    o_ref[c0:c0 + MOE_CHUNK, :] += dot(unperm, ys)


def moe_block(x, g, wr_hi, wr_lo, br, w_gate, w_up, w_down):
    T, D = x.shape
    E, _, Fd = w_gate.shape
    tm = MOE_TM
    row = lambda i, e: (i, 0)
    fixed = lambda i, e: (0, 0)
    return pl.pallas_call(
        _moe_kernel,
        grid=(T // tm, E),
        in_specs=[
            pl.BlockSpec((tm, D), row, pipeline_mode=pl.Buffered(1)),
            pl.BlockSpec((1, D), fixed),
            pl.BlockSpec((D, LANES), fixed),
            pl.BlockSpec((D, LANES), fixed),
            pl.BlockSpec((1, LANES), fixed),
            pl.BlockSpec((None, D, Fd), lambda i, e: (e, 0, 0)),
            pl.BlockSpec((None, D, Fd), lambda i, e: (e, 0, 0)),
            pl.BlockSpec((None, Fd, D), lambda i, e: (e, 0, 0)),
        ],
        out_specs=pl.BlockSpec((tm, D), row, pipeline_mode=pl.Buffered(1)),
        out_shape=jax.ShapeDtypeStruct((T, D), F32),
        scratch_shapes=[pltpu.VMEM((MOE_ROWS, D), BF16), pltpu.VMEM((MOE_ROWS, D), BF16),
                        pltpu.VMEM((MOE_ROWS, LANES), F32), pltpu.VMEM((tm, LANES), F32),
                        pltpu.SMEM((2 * N_GROUPS + 1,), jnp.int32)],
        compiler_params=_cparams(("parallel", "arbitrary")),
        name="moe_block",
    )(x, g, wr_hi, wr_lo, br, w_gate, w_up, w_down)


def _prep_route(w_route_group, b_route_group, w_route_expert, b_route_expert):
    D = w_route_group.shape[0]
    pad = LANES - N_GROUPS - N_EXPERTS
    wr = jnp.concatenate([w_route_group, w_route_expert, jnp.zeros((D, pad), F32)], axis=1)
    br = jnp.concatenate([b_route_group, b_route_expert, jnp.zeros((pad,), F32)])[None]
    hi = wr.astype(BF16)
    lo = (wr - hi.astype(F32)).astype(BF16)
    return hi, lo, br


def _prep_lora(w_decay_up, w_aicl_up, w_gate_lora_up):
    zw = jnp.zeros((LANES - DECAY_LORA, RWKV_WIDTH), w_decay_up.dtype)
    wd = jnp.concatenate([w_decay_up, zw], axis=0)
    wa = jnp.concatenate([jnp.zeros((DECAY_LORA, RWKV_WIDTH), w_aicl_up.dtype), w_aicl_up], axis=0)
    wg = jnp.concatenate([w_gate_lora_up,
                          jnp.zeros((2 * LANES - GATE_LORA, RWKV_WIDTH), w_gate_lora_up.dtype)], axis=0)
    return wd.astype(BF16), wa.astype(BF16), wg.astype(BF16)


DSA_COLS = 4176
IN_PROJ_TILES = 2
IN_PROJ_ROWS = 512
IN_PROJ_ROWS_NARROW = 1024
RWKV_SPLITS = (RWKV_WIDTH, DECAY_LORA, RWKV_WIDTH, RWKV_WIDTH, AICL_LORA, GATE_LORA)


def _rwkv_cols(t):
    cuts = np.cumsum(RWKV_SPLITS)[:-1].tolist()
    r, dw, k, v, da, dg = jnp.split(t, cuts, axis=-1)
    pad = jnp.zeros(t.shape[:-1] + (RWKV_PAD - sum(RWKV_SPLITS),), t.dtype)
    return jnp.concatenate([r, k, v, dw, da, dg, pad], axis=-1)


def _prep_in_weights(w_in):
    pad = jnp.zeros((w_in.shape[0], DSA_PAD - DSA_COLS), w_in.dtype)
    w_dsa = jnp.concatenate([w_in[:, :DSA_COLS], pad], axis=1)
    w_rwkv = _rwkv_cols(w_in[:, DSA_COLS:])

    def tiles(w):
        d, n = w.shape
        return w.astype(BF16).reshape(d, IN_PROJ_TILES, n // IN_PROJ_TILES).transpose(1, 0, 2)

    return tiles(w_dsa), tiles(w_rwkv)


def kernel(x, g_mix, w_in, rwkv_shift_mix, q_gain, k_gain, w0, w_decay_up, a0, w_aicl_up, w_gate_lora_up, k_k, k_a, r_k, ln_x_w, ln_x_b, w_out, g_ffn, w_route_group, b_route_group, w_route_expert, b_route_expert, w_e_gate, w_e_up, w_e_down):
    B, S, D = x.shape
    T = B * S
    depth = g_mix.shape[0]
    xf = x.reshape(T, D)
    vec = lambda t: t.reshape(1, -1)
    for l in range(depth):
        w_dsa, w_rwkv = _prep_in_weights(w_in[l])
        p_dsa = norm_matmul(xf, vec(g_mix[l]), w_dsa, IN_PROJ_ROWS)
        p_rwkv = norm_matmul(xf, vec(g_mix[l]), w_rwkv, IN_PROJ_ROWS_NARROW)
        q, k, v, qi, ki, w = dsa_prep(p_dsa, vec(q_gain[l]), vec(k_gain[l]), B, S, DSA_TILE)
        y_dsa = dsa_attention(q, k, v, qi, ki, w, B, S, DSA_TILE, DSA_TILE)
        wd, wa, wg = _prep_lora(w_decay_up[l], w_aicl_up[l], w_gate_lora_up[l])
        y_rwkv = rwkv_mixer(p_rwkv, _rwkv_cols(vec(rwkv_shift_mix[l])), vec(w0[l]), vec(a0[l]),
                            vec(k_k[l]), vec(k_a[l]), vec(r_k[l]), vec(ln_x_w[l]), vec(ln_x_b[l]),
                            wd, wa, wg, B, S, CHUNK)
        xf = out_proj(xf, y_dsa, y_rwkv, w_out[l].astype(BF16), 512)
        wr_hi, wr_lo, br = _prep_route(w_route_group[l], b_route_group[l],
                                       w_route_expert[l], b_route_expert[l])
        xf = moe_block(xf, vec(g_ffn[l]), wr_hi, wr_lo, br, w_e_gate[l].astype(BF16),
                       w_e_up[l].astype(BF16), w_e_down[l].astype(BF16))
    return xf.reshape(B, S, D)
```
